```python
import math
import jax, jax.numpy as jnp
from jax import lax
import numpy as np

D_MODEL = 1024
BATCH = 4
SEQ = 8192
DEPTH = 4

CONV_WIDTH = D_MODEL // 2
CONV_K = 3
RWKV_WIDTH = D_MODEL - CONV_WIDTH
HEAD_SIZE = 64
RWKV_HEADS = RWKV_WIDTH // HEAD_SIZE
D_DECAY_LORA = 64
D_AAA_LORA = 64
D_GATE_LORA = 128
D_FF = -(-8 * D_MODEL // (3 * 256)) * 256
N_MOD = 6
DEEPNORM_ALPHA = (2.0 * DEPTH) ** 0.25
DEEPNORM_BETA = (8.0 * DEPTH) ** -0.25
LN_EPS = 1e-5
GN_EPS = 64e-5

N_CONV_COLS = 3 * CONV_WIDTH
N_SHIFT_COLS = 3 * RWKV_WIDTH + D_DECAY_LORA + D_AAA_LORA + D_GATE_LORA
N_IN_COLS = N_CONV_COLS + N_SHIFT_COLS

kernel_name = "hymba_conv_rwkv7_deepnorm_adaln"


def layer_norm(x, g, b):
    xf = x.astype(jnp.float32)
    mu = jnp.mean(xf, axis=-1, keepdims=True)
    var = jnp.mean(jnp.square(xf - mu), axis=-1, keepdims=True)
    y = (xf - mu) * lax.rsqrt(var + LN_EPS)
    return (y * g.astype(jnp.float32) + b.astype(jnp.float32)).astype(x.dtype)


def token_shift(p):
    return jnp.pad(p, ((0, 0), (1, 0), (0, 0)))[:, :-1, :]


def heads(t):
    return t.reshape(t.shape[:-1] + (RWKV_HEADS, HEAD_SIZE))


def short_conv_mixer(p_conv, conv_w):
    b_gate, c_gate, u = jnp.split(p_conv, 3, axis=-1)
    z = c_gate * u
    zc = lax.conv_general_dilated(
        z, conv_w.astype(z.dtype)[:, None, :], window_strides=(1,),
        padding=[(CONV_K - 1, 0)],
        dimension_numbers=('NWC', 'WIO', 'NWC'),
        feature_group_count=CONV_WIDTH)
    return b_gate * zc


def rwkv7_scan(r, decay, k, v, a_vec, b_vec):
    def step(state, inp):
        r_t, w_t, k_t, v_t, a_t, b_t = inp
        sa = jnp.einsum('bhvk,bhk->bhv', state, a_t)
        state = (state * w_t[:, :, None, :]
                 + sa[..., None] * b_t[:, :, None, :]
                 + v_t[..., None] * k_t[:, :, None, :])
        y_t = jnp.einsum('bhvk,bhk->bhv', state, r_t)
        return state, y_t
    xs = tuple(jnp.moveaxis(t, 1, 0) for t in (r, decay, k, v, a_vec, b_vec))
    bsz = r.shape[0]
    s0 = jnp.zeros((bsz, RWKV_HEADS, HEAD_SIZE, HEAD_SIZE), jnp.float32)
    _, ys = lax.scan(step, s0, xs)
    return jnp.moveaxis(ys, 0, 1)


def rwkv7_mixer(p_shift, mu_shift, w0, w_decay_up, a0, a_up, g_up, k_k, k_a, r_k, lnx_g, lnx_b):
    p = p_shift + mu_shift * (token_shift(p_shift) - p_shift)
    i1 = RWKV_WIDTH
    i2 = 2 * RWKV_WIDTH
    i3 = 3 * RWKV_WIDTH
    i4 = i3 + D_DECAY_LORA
    i5 = i4 + D_AAA_LORA
    r, k, v, w_lo, a_lo, g_lo = jnp.split(p, [i1, i2, i3, i4, i5], axis=-1)
    w_log = -jax.nn.softplus(-(w0 + jnp.tanh(w_lo) @ w_decay_up)) - 0.5
    decay = jnp.exp(-jnp.exp(w_log.astype(jnp.float32)))
    a = jax.nn.sigmoid(a0 + a_lo @ a_up)
    g = jax.nn.sigmoid(g_lo) @ g_up
    kk = heads((k * k_k).astype(jnp.float32))
    kk = kk * lax.rsqrt(jnp.maximum(jnp.sum(jnp.square(kk), -1, keepdims=True), 1e-24))
    k = k * (1.0 + (a - 1.0) * k_a)
    rh = heads(r).astype(jnp.float32)
    kh = heads(k).astype(jnp.float32)
    vh = heads(v).astype(jnp.float32)
    ah = heads(a).astype(jnp.float32)
    y = rwkv7_scan(rh, heads(decay), kh, vh, -kk, kk * ah)
    mu = jnp.mean(y, -1, keepdims=True)
    var = jnp.mean(jnp.square(y - mu), -1, keepdims=True)
    yn = (y - mu) * lax.rsqrt(var + GN_EPS)
    yn = yn.reshape(y.shape[:-2] + (RWKV_WIDTH,)) * lnx_g + lnx_b
    bonus = jnp.sum(rh * kh * r_k.astype(jnp.float32), -1, keepdims=True) * vh
    out = yn + bonus.reshape(y.shape[:-2] + (RWKV_WIDTH,))
    return (out * g).astype(p_shift.dtype)


def setup_inputs(seed: int = 0) -> dict:
    key = jax.random.key(seed)
    ks = jax.random.split(key, 24)

    def nrm(k, shape, scale):
        return jax.random.normal(k, shape, jnp.float32) * scale

    L, D = DEPTH, D_MODEL
    return {
        "x": nrm(ks[0], (BATCH, SEQ, D), 1.0),
        "c": nrm(ks[1], (BATCH, D), 1.0),
        "w_mod": nrm(ks[2], (L, D, N_MOD * D), 0.1 * D ** -0.5),
        "b_mod": nrm(ks[3], (L, N_MOD * D), 0.02),
        "w_in": nrm(ks[4], (L, D, N_IN_COLS), D ** -0.5),
        "conv_w": nrm(ks[5], (L, CONV_K, CONV_WIDTH), CONV_K ** -0.5),
        "mu_shift": jax.random.uniform(ks[6], (L, N_SHIFT_COLS), jnp.float32),
        "w0": jax.random.uniform(ks[7], (L, RWKV_WIDTH), jnp.float32, -5.0, -1.0),
        "w_decay_up": nrm(ks[8], (L, D_DECAY_LORA, RWKV_WIDTH), 0.1 * D_DECAY_LORA ** -0.5),
        "a0": nrm(ks[9], (L, RWKV_WIDTH), 0.1),
        "a_up": nrm(ks[10], (L, D_AAA_LORA, RWKV_WIDTH), 0.1 * D_AAA_LORA ** -0.5),
        "g_up": nrm(ks[11], (L, D_GATE_LORA, RWKV_WIDTH), D_GATE_LORA ** -0.5),
        "k_k": 0.85 + nrm(ks[12], (L, RWKV_WIDTH), 0.02),
        "k_a": 1.0 + nrm(ks[13], (L, RWKV_WIDTH), 0.02),
        "r_k": nrm(ks[14], (L, RWKV_HEADS, HEAD_SIZE), 0.1),
        "lnx_g": 1.0 + nrm(ks[15], (L, RWKV_WIDTH), 0.02),
        "lnx_b": nrm(ks[16], (L, RWKV_WIDTH), 0.02),
        "w_out": nrm(ks[17], (L, D, D), DEEPNORM_BETA * D ** -0.5),
        "ln1_g": 1.0 + nrm(ks[18], (L, D), 0.02),
        "ln1_b": nrm(ks[19], (L, D), 0.02),
        "w_ffn_in": nrm(ks[20], (L, D, 2 * D_FF), D ** -0.5),
        "w_ffn_out": nrm(ks[21], (L, D_FF, D), DEEPNORM_BETA * D_FF ** -0.5),
        "ln2_g": 1.0 + nrm(ks[22], (L, D), 0.02),
        "ln2_b": nrm(ks[23], (L, D), 0.02),
    }


def reference(x, c, w_mod, b_mod, w_in, conv_w, mu_shift, w0, w_decay_up, a0, a_up, g_up,
              k_k, k_a, r_k, lnx_g, lnx_b, w_out, ln1_g, ln1_b, w_ffn_in, w_ffn_out,
              ln2_g, ln2_b):
    silu_c = jax.nn.silu(c)
    for l in range(DEPTH):
        mod = silu_c @ w_mod[l] + b_mod[l]
        sh1, sc1, gt1, sh2, sc2, gt2 = [m[:, None, :] for m in jnp.split(mod, N_MOD, axis=-1)]

        h = x * (1.0 + sc1) + sh1
        p = h @ w_in[l]
        y_conv = short_conv_mixer(p[..., :N_CONV_COLS], conv_w[l])
        y_rwkv = rwkv7_mixer(p[..., N_CONV_COLS:], mu_shift[l], w0[l], w_decay_up[l], a0[l],
                             a_up[l], g_up[l], k_k[l], k_a[l], r_k[l], lnx_g[l], lnx_b[l])
        mix = jnp.concatenate([y_conv, y_rwkv], axis=-1) @ w_out[l]
        x = layer_norm(DEEPNORM_ALPHA * x + (1.0 + gt1) * mix, ln1_g[l], ln1_b[l])

        h = x * (1.0 + sc2) + sh2
        gate, up = jnp.split(h @ w_ffn_in[l], 2, axis=-1)
        f = (jax.nn.silu(gate) * up) @ w_ffn_out[l]
        x = layer_norm(DEEPNORM_ALPHA * x + (1.0 + gt2) * f, ln2_g[l], ln2_b[l])
    return x
```

```python
import functools

import jax
import jax.numpy as jnp
from jax import lax
from jax.experimental import pallas as pl
from jax.experimental.pallas import tpu as pltpu

F32 = jnp.float32
BF16 = jnp.bfloat16

D_MODEL = 1024
DEPTH = 4
CONV_WIDTH = 512
RWKV_WIDTH = 512
HEAD_SIZE = 64
N_HEADS = RWKV_WIDTH // HEAD_SIZE
LORA_WA = 128
LORA_G = 128
D_FF = 2816
N_MOD = 6
N_CONV_COLS = 3 * CONV_WIDTH
N_SHIFT_COLS = 3 * RWKV_WIDTH + LORA_WA + LORA_G
N_IN_COLS = N_CONV_COLS + N_SHIFT_COLS
ALPHA = (2.0 * DEPTH) ** 0.25
LN_EPS = 1e-5
GN_EPS = 64e-5

CHUNK = 64
GROUP = 256
HEADS_PER_GROUP = GROUP // HEAD_SIZE
N_GROUPS = RWKV_WIDTH // GROUP
SUBLANES = 8

TS_IN = 512
TS_RWKV = 512
TS_OUT = 256
MOD_TN = 1536
VMEM_LIMIT = 56 * 1024 * 1024


def _nn(a, b):
    return lax.dot_general(a, b, (((1,), (0,)), ((), ())), preferred_element_type=F32)


def _nt(a, b):
    return lax.dot_general(a, b, (((1,), (1,)), ((), ())), preferred_element_type=F32)


def _tn(a, b):
    return lax.dot_general(a, b, (((0,), (0,)), ((), ())), preferred_element_type=F32)


def _sigmoid(x):
    return 1.0 / (1.0 + jnp.exp(-x))


def _softplus(x):
    return jnp.maximum(x, 0.0) + jnp.log(1.0 + jnp.exp(-jnp.abs(x)))


def _layer_norm(x, g, b):
    mu = jnp.mean(x, axis=-1, keepdims=True)
    d = x - mu
    var = jnp.mean(d * d, axis=-1, keepdims=True)
    return d * lax.rsqrt(var + LN_EPS) * g + b


def _mod_kernel(c_ref, w_ref, b_ref, o_ref):
    c = c_ref[...]
    s = c * _sigmoid(c)
    o_ref[0] = lax.dot_general(s, w_ref[0], (((1,), (0,)), ((), ())), precision=lax.Precision.HIGHEST,
                               preferred_element_type=F32) + b_ref[0]


def _modulation(c, w_mod, b_mod):
    bsz, d = c.shape
    depth, _, n = w_mod.shape
    rows = -(-bsz // SUBLANES) * SUBLANES
    c_pad = jnp.zeros((rows, d), F32).at[:bsz].set(c)
    out = pl.pallas_call(
        _mod_kernel,
        grid=(depth, n // MOD_TN),
        in_specs=[
            pl.BlockSpec((rows, d), lambda l, j: (0, 0)),
            pl.BlockSpec((1, d, MOD_TN), lambda l, j: (l, 0, j)),
            pl.BlockSpec((1, 1, MOD_TN), lambda l, j: (l, 0, j)),
        ],
        out_specs=pl.BlockSpec((1, rows, MOD_TN), lambda l, j: (l, 0, j)),
        out_shape=jax.ShapeDtypeStruct((depth, rows, n), F32),
        compiler_params=pltpu.CompilerParams(dimension_semantics=("arbitrary", "arbitrary"),
                                             vmem_limit_bytes=VMEM_LIMIT),
        name="modulation",
    )(c_pad, w_mod, b_mod.reshape(depth, 1, n))
    return out[:, :bsz]


def _inproj_kernel(x_ref, sh_ref, sc_ref, win_ref, convw_ref, mu_ref, w0_ref, a0_ref, kk_ref, ka_ref,
                   wd_ref, au_ref, gu_ref, ones_ref,
                   yconv_ref, r_ref, lw_ref, k_ref, v_ref, kkn_ref, bv_ref, g_ref,
                   pbuf, zbuf):
    ts = x_ref.shape[1]

    @pl.when(pl.program_id(1) == 0)
    def _():
        pbuf[0:SUBLANES, :] = jnp.zeros((SUBLANES, N_SHIFT_COLS), F32)
        zbuf[0:SUBLANES, :] = jnp.zeros((SUBLANES, CONV_WIDTH), F32)

    h = x_ref[0] * (1.0 + sc_ref[0]) + sh_ref[0]
    p = _nn(h.astype(BF16), win_ref[...])

    bg = p[:, 0:CONV_WIDTH]
    z = p[:, CONV_WIDTH:2 * CONV_WIDTH] * p[:, 2 * CONV_WIDTH:3 * CONV_WIDTH]
    zbuf[SUBLANES:SUBLANES + ts, :] = z
    z1 = zbuf[SUBLANES - 1:SUBLANES - 1 + ts, :]
    z2 = zbuf[SUBLANES - 2:SUBLANES - 2 + ts, :]
    cw = convw_ref[...]
    yconv_ref[0] = bg * (cw[0:1] * z2 + cw[1:2] * z1 + cw[2:3] * z)
    zbuf[0:SUBLANES, :] = zbuf[ts:ts + SUBLANES, :]

    ps_raw = p[:, N_CONV_COLS:N_IN_COLS]
    pbuf[SUBLANES:SUBLANES + ts, :] = ps_raw
    prev = pbuf[SUBLANES - 1:SUBLANES - 1 + ts, :]
    ps = ps_raw + mu_ref[...] * (prev - ps_raw)
    pbuf[0:SUBLANES, :] = pbuf[ts:ts + SUBLANES, :]

    w3 = 3 * RWKV_WIDTH
    r = ps[:, 0:RWKV_WIDTH]
    k = ps[:, RWKV_WIDTH:2 * RWKV_WIDTH]
    v = ps[:, 2 * RWKV_WIDTH:w3]
    wa = ps[:, w3:w3 + LORA_WA]
    gl = ps[:, w3 + LORA_WA:w3 + LORA_WA + LORA_G]

    dw = _nn(jnp.tanh(wa).astype(BF16), wd_ref[...])
    da = _nn(wa.astype(BF16), au_ref[...])
    w_log = -_softplus(-(w0_ref[...] + dw)) - 0.5
    lw = -jnp.exp(w_log)
    a = _sigmoid(a0_ref[...] + da)
    g = _nn(_sigmoid(gl).astype(BF16), gu_ref[...])

    kkv = k * kk_ref[...]
    ss = _nn((kkv * kkv).astype(BF16), ones_ref[...])
    kkn = kkv * lax.rsqrt(jnp.maximum(ss, 1e-24))
    kmod = k * (1.0 + (a - 1.0) * ka_ref[...])

    r_ref[0] = r
    lw_ref[0] = lw
    k_ref[0] = kmod
    v_ref[0] = v
    kkn_ref[0] = kkn
    bv_ref[0] = kkn * a
    g_ref[0] = g


def _inproj(x, sh1, sc1, win, convw, mu, w0, a0, k_k, k_a, wd, au, gu, ones_bd):
    bsz, seq, d = x.shape
    ts = min(TS_IN, seq)
    const = lambda shape: pl.BlockSpec(shape, lambda b, t: (0,) * len(shape), pipeline_mode=pl.Buffered(1))
    row = pl.BlockSpec((1, ts, RWKV_WIDTH), lambda b, t: (b, t, 0))
    out_sds = jax.ShapeDtypeStruct((bsz, seq, RWKV_WIDTH), F32)
    return pl.pallas_call(
        _inproj_kernel,
        grid=(bsz, seq // ts),
        in_specs=[
            pl.BlockSpec((1, ts, d), lambda b, t: (b, t, 0)),
            pl.BlockSpec((1, 1, d), lambda b, t: (b, 0, 0)),
            pl.BlockSpec((1, 1, d), lambda b, t: (b, 0, 0)),
            const((d, N_IN_COLS)),
            const((3, CONV_WIDTH)),
            const((1, N_SHIFT_COLS)),
            const((1, RWKV_WIDTH)), const((1, RWKV_WIDTH)), const((1, RWKV_WIDTH)), const((1, RWKV_WIDTH)),
            const((LORA_WA, RWKV_WIDTH)), const((LORA_WA, RWKV_WIDTH)), const((LORA_G, RWKV_WIDTH)),
            const((RWKV_WIDTH, RWKV_WIDTH)),
        ],
        out_specs=[row] * 8,
        out_shape=[out_sds] * 8,
        scratch_shapes=[pltpu.VMEM((ts + SUBLANES, N_SHIFT_COLS), F32),
                        pltpu.VMEM((ts + SUBLANES, CONV_WIDTH), F32)],
        compiler_params=pltpu.CompilerParams(dimension_semantics=("arbitrary", "arbitrary"),
                                             vmem_limit_bytes=VMEM_LIMIT),
        name="inproj",
    )(x, sh1, sc1, win, convw, mu, w0, a0, k_k, k_a, wd, au, gu, ones_bd)


def _rwkv_kernel(r_ref, lw_ref, k_ref, v_ref, kkn_ref, bv_ref, g_ref, rk_ref, lng_ref, lnb_ref, ones_ref,
                 y_ref, s_ref):
    ts = r_ref.shape[1]
    c = CHUNK

    @pl.when(pl.program_id(1) == 0)
    def _():
        s_ref[...] = jnp.zeros(s_ref.shape, F32)

    rr = lax.broadcasted_iota(jnp.int32, (GROUP, GROUP), 0) // HEAD_SIZE
    cc = lax.broadcasted_iota(jnp.int32, (GROUP, GROUP), 1) // HEAD_SIZE
    bd = rr == cc
    ti = lax.broadcasted_iota(jnp.int32, (c, GROUP), 0)
    tj = lax.broadcasted_iota(jnp.int32, (c, GROUP), 1) % c
    strict = ti > tj
    incl = ti >= tj
    eye = (ti == tj).astype(F32)

    def level_mask(s):
        return (ti // (2 * s) == tj // (2 * s)) & (ti % (2 * s) >= s) & (tj % (2 * s) < s)

    li = lax.broadcasted_iota(jnp.int32, (c, c), 0)
    lj = lax.broadcasted_iota(jnp.int32, (c, c), 1)
    ltri = (li >= lj).astype(BF16)

    def vt(x):
        xb = x.astype(BF16)
        return jnp.where(bd, jnp.concatenate([xb] * HEADS_PER_GROUP, axis=0), jnp.zeros((), BF16))

    ones = ones_ref[...]
    inv_n = 1.0 / HEAD_SIZE

    def chunk(ci, carry):
        off = pl.multiple_of(ci * c, c)
        rows = pl.ds(off, c)
        r = r_ref[0, rows, :]
        lw = lw_ref[0, rows, :]
        k = k_ref[0, rows, :]
        v = v_ref[0, rows, :]
        kkn = kkn_ref[0, rows, :]
        bv = bv_ref[0, rows, :]

        hi = lw.astype(BF16)
        r1 = lw - hi.astype(F32)
        mid = r1.astype(BF16)
        lo = (r1 - mid.astype(F32)).astype(BF16)
        cl = _nn(ltri, hi) + _nn(ltri, mid) + _nn(ltri, lo)
        cl_end = cl[c - 1:c, :]
        e_in = jnp.exp(cl)
        e_neg = jnp.exp(-cl)
        e_end = jnp.exp(cl_end - cl)
        rt = r * e_in
        at = -kkn * jnp.exp(cl - lw)
        kt = k * e_neg
        bt = bv * e_neg
        kh = k * e_end
        bh = bv * e_end
        w_end = jnp.exp(cl_end)

        ys = []
        for gi in range(N_GROUPS):
            ln = slice(gi * GROUP, (gi + 1) * GROUP)
            at_g = at[:, ln]
            v_g = v[:, ln]
            ar = jnp.concatenate([at_g, rt[:, ln]], axis=0).astype(BF16)
            pb = _nt(ar, vt(bt[:, ln]))
            pk = _nt(ar, vt(kt[:, ln]))
            a_ab = jnp.where(strict, pb[0:c], 0.0)
            a_ak = jnp.where(strict, pk[0:c], 0.0)
            m_rb = jnp.where(incl, pb[c:2 * c], 0.0)
            m_rk = jnp.where(incl, pk[c:2 * c], 0.0)

            t_inv = eye + jnp.where(level_mask(1), a_ab, 0.0)
            s = 2
            while s < c:
                am = jnp.where(level_mask(s), a_ab, 0.0).astype(BF16)
                x = _nn(am, vt(t_inv))
                t_inv = t_inv + _nn(t_inv.astype(BF16), vt(x))
                s *= 2
            tb = t_inv.astype(BF16)

            av = _nn(a_ak.astype(BF16), vt(v_g))
            om = _nn(tb, vt(at_g))
            u = _nn(tb, vt(av))

            s_g = s_ref[gi]
            sb = s_g.astype(BF16)
            sa = _nt(om.astype(BF16), sb) + u
            y = _nt(rt[:, ln].astype(BF16), sb) + _nn(m_rb.astype(BF16), vt(sa)) + _nn(m_rk.astype(BF16), vt(v_g))
            lhs = jnp.concatenate([sa, v_g], axis=0).astype(BF16)
            rhs = jnp.concatenate([bh[:, ln], kh[:, ln]], axis=0).astype(BF16)
            upd = _tn(lhs, rhs)
            s_ref[gi] = s_g * w_end[:, ln] + jnp.where(bd, upd, 0.0)
            ys.append(y)

        y = jnp.concatenate(ys, axis=1)
        mu = _nn(y.astype(BF16), ones) * inv_n
        d = y - mu
        var = _nn((d * d).astype(BF16), ones) * inv_n
        yn = d * lax.rsqrt(var + GN_EPS) * lng_ref[...] + lnb_ref[...]
        bonus = _nn((r * k * rk_ref[...]).astype(BF16), ones) * v
        y_ref[0, rows, :] = (yn + bonus) * g_ref[0, rows, :]
        return carry

    lax.fori_loop(0, ts // c, chunk, 0)


def _rwkv(r, lw, k, v, kkn, bv, g, r_k, lnx_g, lnx_b, ones_bd):
    bsz, seq, w = r.shape
    ts = min(TS_RWKV, seq)
    row = pl.BlockSpec((1, ts, w), lambda b, t: (b, t, 0))
    const = lambda shape: pl.BlockSpec(shape, lambda b, t: (0,) * len(shape), pipeline_mode=pl.Buffered(1))
    return pl.pallas_call(
        _rwkv_kernel,
        grid=(bsz, seq // ts),
        in_specs=[row] * 7 + [const((1, w))] * 3 + [const((w, w))],
        out_specs=row,
        out_shape=jax.ShapeDtypeStruct((bsz, seq, w), F32),
        scratch_shapes=[pltpu.VMEM((N_GROUPS, GROUP, GROUP), F32)],
        compiler_params=pltpu.CompilerParams(dimension_semantics=("arbitrary", "arbitrary"),
                                             vmem_limit_bytes=VMEM_LIMIT),
        name="rwkv7_scan",
    )(r, lw, k, v, kkn, bv, g, r_k, lnx_g, lnx_b, ones_bd)


def _out_ffn_kernel(x_ref, yc_ref, yr_ref, gt1_ref, sh2_ref, sc2_ref, gt2_ref, wout_ref, g1_ref, b1_ref,
                    wfi_ref, wfo_ref, g2_ref, b2_ref, o_ref):
    mix = (_nn(yc_ref[0].astype(BF16), wout_ref[0:CONV_WIDTH, :])
           + _nn(yr_ref[0].astype(BF16), wout_ref[CONV_WIDTH:CONV_WIDTH + RWKV_WIDTH, :]))
    x1 = _layer_norm(ALPHA * x_ref[0] + (1.0 + gt1_ref[0]) * mix, g1_ref[...], b1_ref[...])
    hb = (x1 * (1.0 + sc2_ref[0]) + sh2_ref[0]).astype(BF16)
    gate = _nn(hb, wfi_ref[:, 0:D_FF])
    up = _nn(hb, wfi_ref[:, D_FF:2 * D_FF])
    f = _nn((gate * _sigmoid(gate) * up).astype(BF16), wfo_ref[...])
    o_ref[0] = _layer_norm(ALPHA * x1 + (1.0 + gt2_ref[0]) * f, g2_ref[...], b2_ref[...])


def _out_ffn(x, yc, yr, gt1, sh2, sc2, gt2, wout, g1, b1, wfi, wfo, g2, b2):
    bsz, seq, d = x.shape
    ts = min(TS_OUT, seq)
    const = lambda shape: pl.BlockSpec(shape, lambda b, t: (0,) * len(shape), pipeline_mode=pl.Buffered(1))
    modv = pl.BlockSpec((1, 1, d), lambda b, t: (b, 0, 0))
    return pl.pallas_call(
        _out_ffn_kernel,
        grid=(bsz, seq // ts),
        in_specs=[
            pl.BlockSpec((1, ts, d), lambda b, t: (b, t, 0)),
            pl.BlockSpec((1, ts, CONV_WIDTH), lambda b, t: (b, t, 0)),
            pl.BlockSpec((1, ts, RWKV_WIDTH), lambda b, t: (b, t, 0)),
            modv, modv, modv, modv,
            const((d, d)), const((1, d)), const((1, d)),
            const((d, 2 * D_FF)), const((D_FF, d)), const((1, d)), const((1, d)),
        ],
        out_specs=pl.BlockSpec((1, ts, d), lambda b, t: (b, t, 0)),
        out_shape=jax.ShapeDtypeStruct((bsz, seq, d), F32),
        compiler_params=pltpu.CompilerParams(dimension_semantics=("parallel", "parallel"),
                                             vmem_limit_bytes=VMEM_LIMIT),
        name="out_ffn",
    )(x, yc, yr, gt1, sh2, sc2, gt2, wout, g1, b1, wfi, wfo, g2, b2)


def kernel(x, c, w_mod, b_mod, w_in, conv_w, mu_shift, w0, w_decay_up, a0, a_up, g_up, k_k, k_a, r_k,
           lnx_g, lnx_b, w_out, ln1_g, ln1_b, w_ffn_in, w_ffn_out, ln2_g, ln2_b):
    depth = w_mod.shape[0]
    d = x.shape[-1]
    mod = _modulation(c, w_mod, b_mod)

    head = jnp.arange(RWKV_WIDTH) // HEAD_SIZE
    ones_bd = (head[:, None] == head[None, :]).astype(BF16)
    zeros_half = jnp.zeros((LORA_WA // 2, RWKV_WIDTH), F32)
    row2 = lambda v: v.reshape(1, -1)

    for l in range(depth):
        m = [mod[l, :, i * d:(i + 1) * d][:, None, :] for i in range(N_MOD)]
        wd = jnp.concatenate([w_decay_up[l], zeros_half], axis=0).astype(BF16)
        au = jnp.concatenate([zeros_half, a_up[l]], axis=0).astype(BF16)
        yc, r, lw, k, v, kkn, bv, g = _inproj(
            x, m[0], m[1], w_in[l].astype(BF16), conv_w[l], row2(mu_shift[l]), row2(w0[l]), row2(a0[l]),
            row2(k_k[l]), row2(k_a[l]), wd, au, g_up[l].astype(BF16), ones_bd)
        yr = _rwkv(r, lw, k, v, kkn, bv, g, row2(r_k[l]), row2(lnx_g[l]), row2(lnx_b[l]), ones_bd)
        x = _out_ffn(x, yc, yr, m[2], m[3], m[4], m[5], w_out[l].astype(BF16), row2(ln1_g[l]), row2(ln1_b[l]),
                     w_ffn_in[l].astype(BF16), w_ffn_out[l].astype(BF16), row2(ln2_g[l]), row2(ln2_b[l]))
    return x
```

```python
import functools

import jax
import jax.numpy as jnp
from jax import lax
from jax.experimental import pallas as pl
from jax.experimental.pallas import tpu as pltpu

F32 = jnp.float32
BF16 = jnp.bfloat16

D_MODEL = 1024
DEPTH = 4
CONV_WIDTH = 512
RWKV_WIDTH = 512
HEAD_SIZE = 64
N_HEADS = RWKV_WIDTH // HEAD_SIZE
LORA_WA = 128
LORA_G = 128
D_FF = 2816
N_MOD = 6
N_CONV_COLS = 3 * CONV_WIDTH
N_SHIFT_COLS = 3 * RWKV_WIDTH + LORA_WA + LORA_G
N_IN_COLS = N_CONV_COLS + N_SHIFT_COLS
ALPHA = (2.0 * DEPTH) ** 0.25
LN_EPS = 1e-5
GN_EPS = 64e-5

CHUNK = 64
GROUP = 256
HEADS_PER_GROUP = GROUP // HEAD_SIZE
N_GROUPS = RWKV_WIDTH // GROUP
SUBLANES = 8

TS_IN = 512
TS_RWKV = 128
TS_OUT = 256
MOD_TN = 1536
VMEM_LIMIT = 56 * 1024 * 1024


def _nn(a, b):
    return lax.dot_general(a, b, (((1,), (0,)), ((), ())), preferred_element_type=F32)


def _nt(a, b):
    return lax.dot_general(a, b, (((1,), (1,)), ((), ())), preferred_element_type=F32)


def _tn(a, b):
    return lax.dot_general(a, b, (((0,), (0,)), ((), ())), preferred_element_type=F32)


def _sigmoid(x):
    return 1.0 / (1.0 + jnp.exp(-x))


def _softplus(x):
    return jnp.maximum(x, 0.0) + jnp.log(1.0 + jnp.exp(-jnp.abs(x)))


def _layer_norm(x, g, b):
    mu = jnp.mean(x, axis=-1, keepdims=True)
    d = x - mu
    var = jnp.mean(d * d, axis=-1, keepdims=True)
    return d * lax.rsqrt(var + LN_EPS) * g + b


def _mod_kernel(c_ref, w_ref, b_ref, o_ref):
    c = c_ref[...]
    s = c * _sigmoid(c)
    o_ref[0] = lax.dot_general(s, w_ref[0], (((1,), (0,)), ((), ())), precision=lax.Precision.HIGHEST,
                               preferred_element_type=F32) + b_ref[0]


def _modulation(c, w_mod, b_mod):
    bsz, d = c.shape
    depth, _, n = w_mod.shape
    rows = -(-bsz // SUBLANES) * SUBLANES
    c_pad = jnp.zeros((rows, d), F32).at[:bsz].set(c)
    out = pl.pallas_call(
        _mod_kernel,
        grid=(depth, n // MOD_TN),
        in_specs=[
            pl.BlockSpec((rows, d), lambda l, j: (0, 0)),
            pl.BlockSpec((1, d, MOD_TN), lambda l, j: (l, 0, j)),
            pl.BlockSpec((1, 1, MOD_TN), lambda l, j: (l, 0, j)),
        ],
        out_specs=pl.BlockSpec((1, rows, MOD_TN), lambda l, j: (l, 0, j)),
        out_shape=jax.ShapeDtypeStruct((depth, rows, n), F32),
        compiler_params=pltpu.CompilerParams(dimension_semantics=("arbitrary", "arbitrary"),
                                             vmem_limit_bytes=VMEM_LIMIT),
        name="modulation",
    )(c_pad, w_mod, b_mod.reshape(depth, 1, n))
    return out[:, :bsz]


def _inproj_kernel(x_ref, sh_ref, sc_ref, win_ref, convw_ref, mu_ref, w0_ref, a0_ref, kk_ref, ka_ref,
                   wd_ref, au_ref, gu_ref, ones_ref,
                   yconv_ref, r_ref, lw_ref, k_ref, v_ref, kkn_ref, bv_ref, g_ref,
                   pbuf, zbuf):
    ts = x_ref.shape[1]

    @pl.when(pl.program_id(1) == 0)
    def _():
        pbuf[0:SUBLANES, :] = jnp.zeros((SUBLANES, N_SHIFT_COLS), F32)
        zbuf[0:SUBLANES, :] = jnp.zeros((SUBLANES, CONV_WIDTH), F32)

    h = x_ref[0] * (1.0 + sc_ref[0]) + sh_ref[0]
    p = _nn(h.astype(BF16), win_ref[...])

    bg = p[:, 0:CONV_WIDTH]
    z = p[:, CONV_WIDTH:2 * CONV_WIDTH] * p[:, 2 * CONV_WIDTH:3 * CONV_WIDTH]
    zbuf[SUBLANES:SUBLANES + ts, :] = z
    z1 = zbuf[SUBLANES - 1:SUBLANES - 1 + ts, :]
    z2 = zbuf[SUBLANES - 2:SUBLANES - 2 + ts, :]
    cw = convw_ref[...]
    yconv_ref[0] = bg * (cw[0:1] * z2 + cw[1:2] * z1 + cw[2:3] * z)
    zbuf[0:SUBLANES, :] = zbuf[ts:ts + SUBLANES, :]

    ps_raw = p[:, N_CONV_COLS:N_IN_COLS]
    pbuf[SUBLANES:SUBLANES + ts, :] = ps_raw
    prev = pbuf[SUBLANES - 1:SUBLANES - 1 + ts, :]
    ps = ps_raw + mu_ref[...] * (prev - ps_raw)
    pbuf[0:SUBLANES, :] = pbuf[ts:ts + SUBLANES, :]

    w3 = 3 * RWKV_WIDTH
    r = ps[:, 0:RWKV_WIDTH]
    k = ps[:, RWKV_WIDTH:2 * RWKV_WIDTH]
    v = ps[:, 2 * RWKV_WIDTH:w3]
    wa = ps[:, w3:w3 + LORA_WA]
    gl = ps[:, w3 + LORA_WA:w3 + LORA_WA + LORA_G]

    dw = _nn(jnp.tanh(wa).astype(BF16), wd_ref[...])
    da = _nn(wa.astype(BF16), au_ref[...])
    w_log = -_softplus(-(w0_ref[...] + dw)) - 0.5
    lw = -jnp.exp(w_log)
    a = _sigmoid(a0_ref[...] + da)
    g = _nn(_sigmoid(gl).astype(BF16), gu_ref[...])

    kkv = k * kk_ref[...]
    ss = _nn((kkv * kkv).astype(BF16), ones_ref[...])
    kkn = kkv * lax.rsqrt(jnp.maximum(ss, 1e-24))
    kmod = k * (1.0 + (a - 1.0) * ka_ref[...])

    r_ref[0] = r
    lw_ref[0] = lw
    k_ref[0] = kmod
    v_ref[0] = v
    kkn_ref[0] = kkn
    bv_ref[0] = kkn * a
    g_ref[0] = g


def _inproj(x, sh1, sc1, win, convw, mu, w0, a0, k_k, k_a, wd, au, gu, ones_bd):
    bsz, seq, d = x.shape
    ts = min(TS_IN, seq)
    const = lambda shape: pl.BlockSpec(shape, lambda b, t: (0,) * len(shape), pipeline_mode=pl.Buffered(1))
    row = pl.BlockSpec((1, ts, RWKV_WIDTH), lambda b, t: (b, t, 0))
    out_sds = jax.ShapeDtypeStruct((bsz, seq, RWKV_WIDTH), F32)
    return pl.pallas_call(
        _inproj_kernel,
        grid=(bsz, seq // ts),
        in_specs=[
            pl.BlockSpec((1, ts, d), lambda b, t: (b, t, 0)),
            pl.BlockSpec((1, 1, d), lambda b, t: (b, 0, 0)),
            pl.BlockSpec((1, 1, d), lambda b, t: (b, 0, 0)),
            const((d, N_IN_COLS)),
            const((3, CONV_WIDTH)),
            const((1, N_SHIFT_COLS)),
            const((1, RWKV_WIDTH)), const((1, RWKV_WIDTH)), const((1, RWKV_WIDTH)), const((1, RWKV_WIDTH)),
            const((LORA_WA, RWKV_WIDTH)), const((LORA_WA, RWKV_WIDTH)), const((LORA_G, RWKV_WIDTH)),
            const((RWKV_WIDTH, RWKV_WIDTH)),
        ],
        out_specs=[row] * 8,
        out_shape=[out_sds] * 8,
        scratch_shapes=[pltpu.VMEM((ts + SUBLANES, N_SHIFT_COLS), F32),
                        pltpu.VMEM((ts + SUBLANES, CONV_WIDTH), F32)],
        compiler_params=pltpu.CompilerParams(dimension_semantics=("arbitrary", "arbitrary"),
                                             vmem_limit_bytes=VMEM_LIMIT),
        name="inproj",
    )(x, sh1, sc1, win, convw, mu, w0, a0, k_k, k_a, wd, au, gu, ones_bd)


def _rwkv_kernel(r_ref, lw_ref, k_ref, v_ref, kkn_ref, bv_ref, g_ref, rk_ref, lng_ref, lnb_ref, ones_ref,
                 y_ref, s_ref):
    nb, ts, _ = r_ref.shape
    c = CHUNK
    nck = ts // c

    @pl.when(pl.program_id(0) == 0)
    def _():
        s_ref[...] = jnp.zeros(s_ref.shape, F32)

    rr = lax.broadcasted_iota(jnp.int32, (GROUP, GROUP), 0) // HEAD_SIZE
    cc = lax.broadcasted_iota(jnp.int32, (GROUP, GROUP), 1) // HEAD_SIZE
    bd = rr == cc
    ti = lax.broadcasted_iota(jnp.int32, (c, GROUP), 0)
    tj = lax.broadcasted_iota(jnp.int32, (c, GROUP), 1) % c
    strict = ti > tj
    incl = ti >= tj
    eye = (ti == tj).astype(F32)

    def level_mask(s):
        return (ti // (2 * s) == tj // (2 * s)) & (ti % (2 * s) >= s) & (tj % (2 * s) < s)

    li = lax.broadcasted_iota(jnp.int32, (c, c), 0)
    lj = lax.broadcasted_iota(jnp.int32, (c, c), 1)
    ltri = (li >= lj).astype(BF16)

    def vt(x):
        xb = x.astype(BF16)
        return jnp.where(bd, jnp.concatenate([xb] * HEADS_PER_GROUP, axis=0), jnp.zeros((), BF16))

    blocks = [(b, ci) for b in range(nb) for ci in range(nck)]
    chains = [(b, ci, gi) for (b, ci) in blocks for gi in range(N_GROUPS)]
    lanes = lambda gi: slice(gi * GROUP, (gi + 1) * GROUP)

    prep = {}
    for (b, ci) in blocks:
        rows = slice(ci * c, (ci + 1) * c)
        lw = lw_ref[b, rows, :]
        k = k_ref[b, rows, :]
        kkn = kkn_ref[b, rows, :]
        bv = bv_ref[b, rows, :]
        hi = lw.astype(BF16)
        r1 = lw - hi.astype(F32)
        mid = r1.astype(BF16)
        lo = (r1 - mid.astype(F32)).astype(BF16)
        cl = _nn(ltri, hi) + _nn(ltri, mid) + _nn(ltri, lo)
        cl_end = cl[c - 1:c, :]
        e_neg = jnp.exp(-cl)
        e_end = jnp.exp(cl_end - cl)
        prep[(b, ci)] = dict(
            rt=(r_ref[b, rows, :] * jnp.exp(cl)).astype(BF16),
            at=(-kkn * jnp.exp(cl - lw)).astype(BF16),
            kt=(k * e_neg).astype(BF16),
            bt=(bv * e_neg).astype(BF16),
            kh=(k * e_end).astype(BF16),
            bh=(bv * e_end).astype(BF16),
            v=v_ref[b, rows, :].astype(BF16),
            w_end=jnp.exp(cl_end))

    st = {}
    for ch in chains:
        b, ci, gi = ch
        p = prep[(b, ci)]
        ln = lanes(gi)
        ar = jnp.concatenate([p["at"][:, ln], p["rt"][:, ln]], axis=0)
        pb = _nt(ar, vt(p["bt"][:, ln]))
        pk = _nt(ar, vt(p["kt"][:, ln]))
        st[ch] = dict(
            a_ab=jnp.where(strict, pb[0:c], 0.0),
            a_ak=jnp.where(strict, pk[0:c], 0.0).astype(BF16),
            m_rb=jnp.where(incl, pb[c:2 * c], 0.0).astype(BF16),
            m_rk=jnp.where(incl, pk[c:2 * c], 0.0).astype(BF16),
            vt_v=vt(p["v"][:, ln]))
        st[ch]["t_inv"] = eye + jnp.where(level_mask(1), st[ch]["a_ab"], 0.0)

    s = 2
    while s < c:
        lm = level_mask(s)
        xs = {ch: _nn(jnp.where(lm, st[ch]["a_ab"], 0.0).astype(BF16), vt(st[ch]["t_inv"])) for ch in chains}
        for ch in chains:
            st[ch]["t_inv"] = st[ch]["t_inv"] + _nn(st[ch]["t_inv"].astype(BF16), vt(xs[ch]))
        s *= 2

    for ch in chains:
        d = st[ch]
        d["av"] = _nn(d["a_ak"], d["vt_v"])
        d["yloc"] = _nn(d["m_rk"], d["vt_v"])
    for ch in chains:
        b, ci, gi = ch
        d = st[ch]
        tb = d["t_inv"].astype(BF16)
        d["om"] = _nn(tb, vt(prep[(b, ci)]["at"][:, lanes(gi)])).astype(BF16)
        d["u"] = _nn(tb, vt(d["av"]))

    ys = {}
    for b in range(nb):
        for gi in range(N_GROUPS):
            ln = lanes(gi)
            s_g = s_ref[b, gi]
            for ci in range(nck):
                p = prep[(b, ci)]
                d = st[(b, ci, gi)]
                res = _nt(jnp.concatenate([d["om"], p["rt"][:, ln]], axis=0), s_g.astype(BF16))
                sa = res[0:c] + d["u"]
                ys[(b, ci, gi)] = res[c:2 * c] + _nn(d["m_rb"], vt(sa)) + d["yloc"]
                lhs = jnp.concatenate([sa.astype(BF16), p["v"][:, ln]], axis=0)
                rhs = jnp.concatenate([p["bh"][:, ln], p["kh"][:, ln]], axis=0)
                s_g = s_g * p["w_end"][:, ln] + jnp.where(bd, _tn(lhs, rhs), 0.0)
            s_ref[b, gi] = s_g

    ones = ones_ref[...]
    inv_n = 1.0 / HEAD_SIZE
    y = jnp.concatenate([jnp.concatenate([ys[(b, ci, gi)] for gi in range(N_GROUPS)], axis=1)
                         for (b, ci) in blocks], axis=0)
    allrows = lambda ref: jnp.concatenate([ref[b] for b in range(nb)], axis=0)
    r = allrows(r_ref)
    k = allrows(k_ref)
    v = allrows(v_ref)
    mu = _nn(y.astype(BF16), ones) * inv_n
    dlt = y - mu
    var = _nn((dlt * dlt).astype(BF16), ones) * inv_n
    yn = dlt * lax.rsqrt(var + GN_EPS) * lng_ref[...] + lnb_ref[...]
    bonus = _nn((r * k * rk_ref[...]).astype(BF16), ones) * v
    out = (yn + bonus) * allrows(g_ref)
    for b in range(nb):
        y_ref[b] = out[b * ts:(b + 1) * ts]


def _rwkv(r, lw, k, v, kkn, bv, g, r_k, lnx_g, lnx_b, ones_bd):
    bsz, seq, w = r.shape
    ts = min(TS_RWKV, seq)
    row = pl.BlockSpec((bsz, ts, w), lambda t: (0, t, 0))
    const = lambda shape: pl.BlockSpec(shape, lambda t: (0,) * len(shape), pipeline_mode=pl.Buffered(1))
    return pl.pallas_call(
        _rwkv_kernel,
        grid=(seq // ts,),
        in_specs=[row] * 7 + [const((1, w))] * 3 + [const((w, w))],
        out_specs=row,
        out_shape=jax.ShapeDtypeStruct((bsz, seq, w), F32),
        scratch_shapes=[pltpu.VMEM((bsz, N_GROUPS, GROUP, GROUP), F32)],
        compiler_params=pltpu.CompilerParams(dimension_semantics=("arbitrary",),
                                             vmem_limit_bytes=VMEM_LIMIT),
        name="rwkv7_scan",
    )(r, lw, k, v, kkn, bv, g, r_k, lnx_g, lnx_b, ones_bd)


def _out_ffn_kernel(x_ref, yc_ref, yr_ref, gt1_ref, sh2_ref, sc2_ref, gt2_ref, wout_ref, g1_ref, b1_ref,
                    wfi_ref, wfo_ref, g2_ref, b2_ref, o_ref):
    mix = (_nn(yc_ref[0].astype(BF16), wout_ref[0:CONV_WIDTH, :])
           + _nn(yr_ref[0].astype(BF16), wout_ref[CONV_WIDTH:CONV_WIDTH + RWKV_WIDTH, :]))
    x1 = _layer_norm(ALPHA * x_ref[0] + (1.0 + gt1_ref[0]) * mix, g1_ref[...], b1_ref[...])
    hb = (x1 * (1.0 + sc2_ref[0]) + sh2_ref[0]).astype(BF16)
    gate = _nn(hb, wfi_ref[:, 0:D_FF])
    up = _nn(hb, wfi_ref[:, D_FF:2 * D_FF])
    f = _nn((gate * _sigmoid(gate) * up).astype(BF16), wfo_ref[...])
    o_ref[0] = _layer_norm(ALPHA * x1 + (1.0 + gt2_ref[0]) * f, g2_ref[...], b2_ref[...])


def _out_ffn(x, yc, yr, gt1, sh2, sc2, gt2, wout, g1, b1, wfi, wfo, g2, b2):
    bsz, seq, d = x.shape
    ts = min(TS_OUT, seq)
    const = lambda shape: pl.BlockSpec(shape, lambda b, t: (0,) * len(shape), pipeline_mode=pl.Buffered(1))
    modv = pl.BlockSpec((1, 1, d), lambda b, t: (b, 0, 0))
    return pl.pallas_call(
        _out_ffn_kernel,
        grid=(bsz, seq // ts),
        in_specs=[
            pl.BlockSpec((1, ts, d), lambda b, t: (b, t, 0)),
            pl.BlockSpec((1, ts, CONV_WIDTH), lambda b, t: (b, t, 0)),
            pl.BlockSpec((1, ts, RWKV_WIDTH), lambda b, t: (b, t, 0)),
            modv, modv, modv, modv,
            const((d, d)), const((1, d)), const((1, d)),
            const((d, 2 * D_FF)), const((D_FF, d)), const((1, d)), const((1, d)),
        ],
        out_specs=pl.BlockSpec((1, ts, d), lambda b, t: (b, t, 0)),
        out_shape=jax.ShapeDtypeStruct((bsz, seq, d), F32),
        compiler_params=pltpu.CompilerParams(dimension_semantics=("parallel", "parallel"),
                                             vmem_limit_bytes=VMEM_LIMIT),
        name="out_ffn",
    )(x, yc, yr, gt1, sh2, sc2, gt2, wout, g1, b1, wfi, wfo, g2, b2)


def kernel(x, c, w_mod, b_mod, w_in, conv_w, mu_shift, w0, w_decay_up, a0, a_up, g_up, k_k, k_a, r_k,
           lnx_g, lnx_b, w_out, ln1_g, ln1_b, w_ffn_in, w_ffn_out, ln2_g, ln2_b):
    depth = w_mod.shape[0]
    d = x.shape[-1]
    mod = _modulation(c, w_mod, b_mod)

    head = jnp.arange(RWKV_WIDTH) // HEAD_SIZE
    ones_bd = (head[:, None] == head[None, :]).astype(BF16)
    zeros_half = jnp.zeros((LORA_WA // 2, RWKV_WIDTH), F32)
    row2 = lambda v: v.reshape(1, -1)

    for l in range(depth):
        m = [mod[l, :, i * d:(i + 1) * d][:, None, :] for i in range(N_MOD)]
        wd = jnp.concatenate([w_decay_up[l], zeros_half], axis=0).astype(BF16)
        au = jnp.concatenate([zeros_half, a_up[l]], axis=0).astype(BF16)
        yc, r, lw, k, v, kkn, bv, g = _inproj(
            x, m[0], m[1], w_in[l].astype(BF16), conv_w[l], row2(mu_shift[l]), row2(w0[l]), row2(a0[l]),
            row2(k_k[l]), row2(k_a[l]), wd, au, g_up[l].astype(BF16), ones_bd)
        yr = _rwkv(r, lw, k, v, kkn, bv, g, row2(r_k[l]), row2(lnx_g[l]), row2(lnx_b[l]), ones_bd)
        x = _out_ffn(x, yc, yr, m[2], m[3], m[4], m[5], w_out[l].astype(BF16), row2(ln1_g[l]), row2(ln1_b[l]),
                     w_ffn_in[l].astype(BF16), w_ffn_out[l].astype(BF16), row2(ln2_g[l]), row2(ln2_b[l]))
    return x
```

```python
import functools

import jax
import jax.numpy as jnp
from jax import lax
from jax.experimental import pallas as pl
from jax.experimental.pallas import tpu as pltpu

F32 = jnp.float32
BF16 = jnp.bfloat16

D_MODEL = 1024
DEPTH = 4
CONV_WIDTH = 512
RWKV_WIDTH = 512
HEAD_SIZE = 64
N_HEADS = RWKV_WIDTH // HEAD_SIZE
LORA_WA = 128
LORA_G = 128
D_FF = 2816
N_MOD = 6
N_CONV_COLS = 3 * CONV_WIDTH
N_SHIFT_COLS = 3 * RWKV_WIDTH + LORA_WA + LORA_G
N_IN_COLS = N_CONV_COLS + N_SHIFT_COLS
ALPHA = (2.0 * DEPTH) ** 0.25
LN_EPS = 1e-5
GN_EPS = 64e-5

CHUNK = 64
GROUP = 256
HEADS_PER_GROUP = GROUP // HEAD_SIZE
N_GROUPS = RWKV_WIDTH // GROUP
SUBLANES = 8

TS_IN = 512
TS_RWKV = 256
TS_OUT = 1024
SUB_ROWS = 256
MXU_WIDTH = 256
FF_BLOCKS = (6 * MXU_WIDTH, 5 * MXU_WIDTH)
MOD_TN = 1536
VMEM_LIMIT = 56 * 1024 * 1024


def _nn(a, b):
    return lax.dot_general(a, b, (((1,), (0,)), ((), ())), preferred_element_type=F32)


def _nt(a, b):
    return lax.dot_general(a, b, (((1,), (1,)), ((), ())), preferred_element_type=F32)


def _tn(a, b):
    return lax.dot_general(a, b, (((0,), (0,)), ((), ())), preferred_element_type=F32)


def _sigmoid(x):
    return 1.0 / (1.0 + jnp.exp(-x))


def _softplus(x):
    return jnp.maximum(x, 0.0) + jnp.log(1.0 + jnp.exp(-jnp.abs(x)))


def _layer_norm(x, g, b):
    mu = jnp.mean(x, axis=-1, keepdims=True)
    d = x - mu
    var = jnp.mean(d * d, axis=-1, keepdims=True)
    return d * lax.rsqrt(var + LN_EPS) * g + b


def _interleave(*stages):
    done = [0] * len(stages)
    alive = [True] * len(stages)
    while any(alive):
        i = min((j for j in range(len(stages)) if alive[j]), key=lambda j: done[j] / stages[j][1])
        try:
            next(stages[i][0])
            done[i] += 1
        except StopIteration:
            alive[i] = False


def _mod_kernel(c_ref, w_ref, b_ref, o_ref):
    c = c_ref[...]
    s = c * _sigmoid(c)
    o_ref[0] = lax.dot_general(s, w_ref[0], (((1,), (0,)), ((), ())), precision=lax.Precision.HIGHEST,
                               preferred_element_type=F32) + b_ref[0]


def _modulation(c, w_mod, b_mod):
    bsz, d = c.shape
    depth, _, n = w_mod.shape
    rows = -(-bsz // SUBLANES) * SUBLANES
    c_pad = jnp.zeros((rows, d), F32).at[:bsz].set(c)
    out = pl.pallas_call(
        _mod_kernel,
        grid=(depth, n // MOD_TN),
        in_specs=[
            pl.BlockSpec((rows, d), lambda l, j: (0, 0)),
            pl.BlockSpec((1, d, MOD_TN), lambda l, j: (l, 0, j)),
            pl.BlockSpec((1, 1, MOD_TN), lambda l, j: (l, 0, j)),
        ],
        out_specs=pl.BlockSpec((1, rows, MOD_TN), lambda l, j: (l, 0, j)),
        out_shape=jax.ShapeDtypeStruct((depth, rows, n), F32),
        compiler_params=pltpu.CompilerParams(dimension_semantics=("arbitrary", "arbitrary"),
                                             vmem_limit_bytes=VMEM_LIMIT),
        name="modulation",
    )(c_pad, w_mod, b_mod.reshape(depth, 1, n))
    return out[:, :bsz]


def _inproj_kernel(x_ref, sh_ref, sc_ref, win_ref, convw_ref, mu_ref, w0_ref, a0_ref, kk_ref, ka_ref,
                   wd_ref, au_ref, gu_ref, ones_ref,
                   yconv_ref, r_ref, lw_ref, k_ref, v_ref, kkn_ref, bv_ref, g_ref,
                   pbuf, zbuf):
    ts = x_ref.shape[1]

    @pl.when(pl.program_id(1) == 0)
    def _():
        pbuf[0:SUBLANES, :] = jnp.zeros((SUBLANES, N_SHIFT_COLS), F32)
        zbuf[0:SUBLANES, :] = jnp.zeros((SUBLANES, CONV_WIDTH), F32)

    sub = min(SUB_ROWS, ts)
    nsub = ts // sub
    ps_ = {}
    w3 = 3 * RWKV_WIDTH

    def stage_project(i):
        rows = slice(i * sub, (i + 1) * sub)
        h = x_ref[0, rows, :] * (1.0 + sc_ref[0]) + sh_ref[0]
        ps_[i] = _nn(h.astype(BF16), win_ref[...])
        yield

    def stage_mixers(i):
        p = ps_[i]
        rows = slice(i * sub, (i + 1) * sub)
        lo = SUBLANES + i * sub
        bg = p[:, 0:CONV_WIDTH]
        z = p[:, CONV_WIDTH:2 * CONV_WIDTH] * p[:, 2 * CONV_WIDTH:3 * CONV_WIDTH]
        zbuf[lo:lo + sub, :] = z
        z1 = zbuf[lo - 1:lo - 1 + sub, :]
        z2 = zbuf[lo - 2:lo - 2 + sub, :]
        cw = convw_ref[...]
        yconv_ref[0, rows, :] = bg * (cw[0:1] * z2 + cw[1:2] * z1 + cw[2:3] * z)
        yield

        ps_raw = p[:, N_CONV_COLS:N_IN_COLS]
        pbuf[lo:lo + sub, :] = ps_raw
        prev = pbuf[lo - 1:lo - 1 + sub, :]
        ps = ps_raw + mu_ref[...] * (prev - ps_raw)
        r = ps[:, 0:RWKV_WIDTH]
        k = ps[:, RWKV_WIDTH:2 * RWKV_WIDTH]
        wa = ps[:, w3:w3 + LORA_WA]
        gl = ps[:, w3 + LORA_WA:w3 + LORA_WA + LORA_G]
        r_ref[0, rows, :] = r
        v_ref[0, rows, :] = ps[:, 2 * RWKV_WIDTH:w3]
        yield

        dw = _nn(jnp.tanh(wa).astype(BF16), wd_ref[...])
        da = _nn(wa.astype(BF16), au_ref[...])
        w_log = -_softplus(-(w0_ref[...] + dw)) - 0.5
        lw_ref[0, rows, :] = -jnp.exp(w_log)
        a = _sigmoid(a0_ref[...] + da)
        g_ref[0, rows, :] = _nn(_sigmoid(gl).astype(BF16), gu_ref[...])
        yield

        kkv = k * kk_ref[...]
        ss = _nn((kkv * kkv).astype(BF16), ones_ref[...])
        kkn = kkv * lax.rsqrt(jnp.maximum(ss, 1e-24))
        k_ref[0, rows, :] = k * (1.0 + (a - 1.0) * ka_ref[...])
        kkn_ref[0, rows, :] = kkn
        bv_ref[0, rows, :] = kkn * a
        yield

    _interleave((stage_project(0), 1))
    for i in range(nsub):
        work = [(stage_mixers(i), 4)]
        if i + 1 < nsub:
            work.append((stage_project(i + 1), 1))
        _interleave(*work)
    zbuf[0:SUBLANES, :] = zbuf[ts:ts + SUBLANES, :]
    pbuf[0:SUBLANES, :] = pbuf[ts:ts + SUBLANES, :]


def _inproj(x, sh1, sc1, win, convw, mu, w0, a0, k_k, k_a, wd, au, gu, ones_bd):
    bsz, seq, d = x.shape
    ts = min(TS_IN, seq)
    const = lambda shape: pl.BlockSpec(shape, lambda b, t: (0,) * len(shape), pipeline_mode=pl.Buffered(1))
    row = pl.BlockSpec((1, ts, RWKV_WIDTH), lambda b, t: (b, t, 0))
    out_sds = jax.ShapeDtypeStruct((bsz, seq, RWKV_WIDTH), F32)
    return pl.pallas_call(
        _inproj_kernel,
        grid=(bsz, seq // ts),
        in_specs=[
            pl.BlockSpec((1, ts, d), lambda b, t: (b, t, 0)),
            pl.BlockSpec((1, 1, d), lambda b, t: (b, 0, 0)),
            pl.BlockSpec((1, 1, d), lambda b, t: (b, 0, 0)),
            const((d, N_IN_COLS)),
            const((3, CONV_WIDTH)),
            const((1, N_SHIFT_COLS)),
            const((1, RWKV_WIDTH)), const((1, RWKV_WIDTH)), const((1, RWKV_WIDTH)), const((1, RWKV_WIDTH)),
            const((LORA_WA, RWKV_WIDTH)), const((LORA_WA, RWKV_WIDTH)), const((LORA_G, RWKV_WIDTH)),
            const((RWKV_WIDTH, RWKV_WIDTH)),
        ],
        out_specs=[row] * 8,
        out_shape=[out_sds] * 8,
        scratch_shapes=[pltpu.VMEM((ts + SUBLANES, N_SHIFT_COLS), F32),
                        pltpu.VMEM((ts + SUBLANES, CONV_WIDTH), F32)],
        compiler_params=pltpu.CompilerParams(dimension_semantics=("arbitrary", "arbitrary"),
                                             vmem_limit_bytes=VMEM_LIMIT),
        name="inproj",
    )(x, sh1, sc1, win, convw, mu, w0, a0, k_k, k_a, wd, au, gu, ones_bd)


def _rwkv_kernel(r_ref, lw_ref, k_ref, v_ref, kkn_ref, bv_ref, g_ref, rk_ref, lng_ref, lnb_ref, ones_ref,
                 y_ref, s_ref):
    nb, ts, _ = r_ref.shape
    c = CHUNK
    nck = ts // c

    @pl.when(pl.program_id(0) == 0)
    def _():
        s_ref[...] = jnp.zeros(s_ref.shape, F32)

    rr = lax.broadcasted_iota(jnp.int32, (GROUP, GROUP), 0) // HEAD_SIZE
    cc = lax.broadcasted_iota(jnp.int32, (GROUP, GROUP), 1) // HEAD_SIZE
    bd = rr == cc
    bd_bf = bd.astype(BF16)
    ti = lax.broadcasted_iota(jnp.int32, (c, GROUP), 0)
    tj = lax.broadcasted_iota(jnp.int32, (c, GROUP), 1) % c
    strict = ti > tj
    incl = ti >= tj
    eye = (ti == tj).astype(F32)

    def level_mask(s):
        return (ti // (2 * s) == tj // (2 * s)) & (ti % (2 * s) >= s) & (tj % (2 * s) < s)

    trow = lax.broadcasted_iota(jnp.int32, (c, RWKV_WIDTH), 0)

    def vt(x):
        xb = x.astype(BF16)
        return jnp.concatenate([xb] * HEADS_PER_GROUP, axis=0) * bd_bf

    blocks = [(b, ci) for b in range(nb) for ci in range(nck)]
    chains = [(b, ci, gi) for (b, ci) in blocks for gi in range(N_GROUPS)]
    lanes = lambda gi: slice(gi * GROUP, (gi + 1) * GROUP)

    prep, st, ys = {}, {}, {}
    seqs = [(b, gi) for b in range(nb) for gi in range(N_GROUPS)]
    state = {sq: s_ref[sq[0], sq[1]] for sq in seqs}


    def stage_operands(ci):
        for b in range(nb):
            rows = slice(ci * c, (ci + 1) * c)
            lw = lw_ref[b, rows, :]
            k = k_ref[b, rows, :]
            kkn = kkn_ref[b, rows, :]
            bv = bv_ref[b, rows, :]
            cl = lw
            dist = 1
            while dist < c:
                cl = cl + jnp.where(trow >= dist, pltpu.roll(cl, dist, axis=0), 0.0)
                dist *= 2
            cl_end = cl[c - 1:c, :]
            e_neg = jnp.exp(-cl)
            e_end = jnp.exp(cl_end - cl)
            p = dict(
                rt=(r_ref[b, rows, :] * jnp.exp(cl)).astype(BF16),
                at=(-kkn * jnp.exp(cl - lw)).astype(BF16),
                kt=(k * e_neg).astype(BF16),
                bt=(bv * e_neg).astype(BF16),
                kh=(k * e_end).astype(BF16),
                bh=(bv * e_end).astype(BF16),
                v=v_ref[b, rows, :].astype(BF16),
                w_end=jnp.exp(cl_end))
            prep[(b, ci)] = p
            yield
            for gi in range(N_GROUPS):
                ln = lanes(gi)
                ar = jnp.concatenate([p["at"][:, ln], p["rt"][:, ln]], axis=0)
                pb = _nt(ar, vt(p["bt"][:, ln]))
                pk = _nt(ar, vt(p["kt"][:, ln]))
                a_ab = jnp.where(strict, pb[0:c], 0.0)
                st[(b, ci, gi)] = dict(
                    a_ab=a_ab,
                    a_ak=jnp.where(strict, pk[0:c], 0.0).astype(BF16),
                    m_rb=jnp.where(incl, pb[c:2 * c], 0.0).astype(BF16),
                    m_rk=jnp.where(incl, pk[c:2 * c], 0.0).astype(BF16),
                    vt_v=vt(p["v"][:, ln]),
                    t_inv=eye + jnp.where(level_mask(1), a_ab, 0.0))
                yield

    def stage_solve(ci):
        chs = [(b, ci, gi) for (b, gi) in seqs]
        s = 2
        while s < c:
            lm = level_mask(s)
            xs = {}
            for ch in chs:
                xs[ch] = _nn(jnp.where(lm, st[ch]["a_ab"], 0.0).astype(BF16), vt(st[ch]["t_inv"]))
                yield
            for ch in chs:
                st[ch]["t_inv"] = st[ch]["t_inv"] + _nn(st[ch]["t_inv"].astype(BF16), vt(xs[ch]))
                yield
            s *= 2
        for ch in chs:
            d = st[ch]
            avy = _nn(jnp.concatenate([d["a_ak"], d["m_rk"]], axis=0), d["vt_v"])
            d["av"] = avy[0:c]
            d["yloc"] = avy[c:2 * c]
            yield
        for ch in chs:
            b, _, gi = ch
            d = st[ch]
            tb = d["t_inv"].astype(BF16)
            d["om"] = _nn(tb, vt(prep[(b, ci)]["at"][:, lanes(gi)])).astype(BF16)
            yield
            d["u"] = _nn(tb, vt(d["av"]))
            yield

    def stage_state(ci):
        res, sa, upd = {}, {}, {}
        for (b, gi) in seqs:
            lhs = jnp.concatenate([st[(b, ci, gi)]["om"], prep[(b, ci)]["rt"][:, lanes(gi)]], axis=0)
            res[(b, gi)] = _nt(lhs, state[(b, gi)].astype(BF16))
            yield
        for sq in seqs:
            sa[sq] = res[sq][0:c] + st[(sq[0], ci, sq[1])]["u"]
        for (b, gi) in seqs:
            p = prep[(b, ci)]
            ln = lanes(gi)
            lhs = jnp.concatenate([sa[(b, gi)].astype(BF16), p["v"][:, ln]], axis=0)
            rhs = jnp.concatenate([p["bh"][:, ln], p["kh"][:, ln]], axis=0)
            upd[(b, gi)] = _tn(lhs, rhs)
            yield
        for (b, gi) in seqs:
            d = st[(b, ci, gi)]
            ys[(b, ci, gi)] = res[(b, gi)][c:2 * c] + _nn(d["m_rb"], vt(sa[(b, gi)])) + d["yloc"]
            state[(b, gi)] = (state[(b, gi)] * prep[(b, ci)]["w_end"][:, lanes(gi)]
                              + jnp.where(bd, upd[(b, gi)], 0.0))
            yield

    n_seq = len(seqs)
    n_levels = 0
    s = 2
    while s < c:
        n_levels += 1
        s *= 2
    len_operands = nb * (1 + N_GROUPS)
    len_solve = n_seq * (2 * n_levels + 3)
    len_state = 3 * n_seq

    ones = ones_ref[...]
    inv_n = 1.0 / HEAD_SIZE

    def stage_output(ci):
        rows = slice(ci * c, (ci + 1) * c)
        y = jnp.concatenate([jnp.concatenate([ys[(b, ci, gi)] for gi in range(N_GROUPS)], axis=1)
                             for b in range(nb)], axis=0)
        allrows = lambda ref: jnp.concatenate([ref[b, rows, :] for b in range(nb)], axis=0)
        mu = _nn(y.astype(BF16), ones) * inv_n
        yield
        dlt = y - mu
        var = _nn((dlt * dlt).astype(BF16), ones) * inv_n
        yield
        yn = dlt * lax.rsqrt(var + GN_EPS) * lng_ref[...] + lnb_ref[...]
        bonus = _nn((allrows(r_ref) * allrows(k_ref) * rk_ref[...]).astype(BF16), ones) * allrows(v_ref)
        yield
        out = (yn + bonus) * allrows(g_ref)
        for b in range(nb):
            y_ref[b, rows, :] = out[b * c:(b + 1) * c]
        yield

    len_output = 4

    _interleave((stage_operands(0), len_operands))
    for ci in range(nck):
        work = [(stage_solve(ci), len_solve)]
        if ci + 1 < nck:
            work.append((stage_operands(ci + 1), len_operands))
        if ci > 0:
            work.append((stage_state(ci - 1), len_state))
        if ci > 1:
            work.append((stage_output(ci - 2), len_output))
        _interleave(*work)
    work = [(stage_state(nck - 1), len_state)]
    if nck > 1:
        work.append((stage_output(nck - 2), len_output))
    _interleave(*work)
    _interleave((stage_output(nck - 1), len_output))
    for (b, gi) in seqs:
        s_ref[b, gi] = state[(b, gi)]


def _rwkv(r, lw, k, v, kkn, bv, g, r_k, lnx_g, lnx_b, ones_bd):
    bsz, seq, w = r.shape
    ts = min(TS_RWKV, seq)
    row = pl.BlockSpec((bsz, ts, w), lambda t: (0, t, 0))
    const = lambda shape: pl.BlockSpec(shape, lambda t: (0,) * len(shape), pipeline_mode=pl.Buffered(1))
    return pl.pallas_call(
        _rwkv_kernel,
        grid=(seq // ts,),
        in_specs=[row] * 7 + [const((1, w))] * 3 + [const((w, w))],
        out_specs=row,
        out_shape=jax.ShapeDtypeStruct((bsz, seq, w), F32),
        scratch_shapes=[pltpu.VMEM((bsz, N_GROUPS, GROUP, GROUP), F32)],
        compiler_params=pltpu.CompilerParams(dimension_semantics=("arbitrary",),
                                             vmem_limit_bytes=VMEM_LIMIT),
        name="rwkv7_scan",
    )(r, lw, k, v, kkn, bv, g, r_k, lnx_g, lnx_b, ones_bd)


def _out_ffn_kernel(x_ref, yc_ref, yr_ref, gt1_ref, sh2_ref, sc2_ref, gt2_ref, wout_ref, g1_ref, b1_ref,
                    wfi_ref, wfo_ref, g2_ref, b2_ref, o_ref):
    ts = x_ref.shape[1]
    sub = min(SUB_ROWS, ts)
    nsub = ts // sub
    x1s, hbs, fs = {}, {}, {}

    def stage_mix(i):
        rows = slice(i * sub, (i + 1) * sub)
        mix = (_nn(yc_ref[0, rows, :].astype(BF16), wout_ref[0:CONV_WIDTH, :])
               + _nn(yr_ref[0, rows, :].astype(BF16), wout_ref[CONV_WIDTH:CONV_WIDTH + RWKV_WIDTH, :]))
        yield
        x1 = _layer_norm(ALPHA * x_ref[0, rows, :] + (1.0 + gt1_ref[0]) * mix, g1_ref[...], b1_ref[...])
        x1s[i] = x1
        hbs[i] = (x1 * (1.0 + sc2_ref[0]) + sh2_ref[0]).astype(BF16)
        yield

    def stage_ffn(i):
        acc = None
        j = 0
        for blk in FF_BLOCKS:
            gate = _nn(hbs[i], wfi_ref[:, j:j + blk])
            up = _nn(hbs[i], wfi_ref[:, D_FF + j:D_FF + j + blk])
            yield
            part = _nn((gate * _sigmoid(gate) * up).astype(BF16), wfo_ref[j:j + blk, :])
            acc = part if acc is None else acc + part
            j += blk
            yield
        fs[i] = acc

    def stage_out(i):
        rows = slice(i * sub, (i + 1) * sub)
        o_ref[0, rows, :] = _layer_norm(ALPHA * x1s[i] + (1.0 + gt2_ref[0]) * fs[i], g2_ref[...], b2_ref[...])
        yield

    len_ffn = 2 * len(FF_BLOCKS)
    _interleave((stage_mix(0), 2))
    for i in range(nsub):
        work = [(stage_ffn(i), len_ffn)]
        if i + 1 < nsub:
            work.append((stage_mix(i + 1), 2))
        if i > 0:
            work.append((stage_out(i - 1), 1))
        _interleave(*work)
    _interleave((stage_out(nsub - 1), 1))


def _out_ffn(x, yc, yr, gt1, sh2, sc2, gt2, wout, g1, b1, wfi, wfo, g2, b2):
    bsz, seq, d = x.shape
    ts = min(TS_OUT, seq)
    const = lambda shape: pl.BlockSpec(shape, lambda b, t: (0,) * len(shape), pipeline_mode=pl.Buffered(1))
    modv = pl.BlockSpec((1, 1, d), lambda b, t: (b, 0, 0))
    return pl.pallas_call(
        _out_ffn_kernel,
        grid=(bsz, seq // ts),
        in_specs=[
            pl.BlockSpec((1, ts, d), lambda b, t: (b, t, 0)),
            pl.BlockSpec((1, ts, CONV_WIDTH), lambda b, t: (b, t, 0)),
            pl.BlockSpec((1, ts, RWKV_WIDTH), lambda b, t: (b, t, 0)),
            modv, modv, modv, modv,
            const((d, d)), const((1, d)), const((1, d)),
            const((d, 2 * D_FF)), const((D_FF, d)), const((1, d)), const((1, d)),
        ],
        out_specs=pl.BlockSpec((1, ts, d), lambda b, t: (b, t, 0)),
        out_shape=jax.ShapeDtypeStruct((bsz, seq, d), F32),
        compiler_params=pltpu.CompilerParams(dimension_semantics=("parallel", "parallel"),
                                             vmem_limit_bytes=VMEM_LIMIT),
        name="out_ffn",
    )(x, yc, yr, gt1, sh2, sc2, gt2, wout, g1, b1, wfi, wfo, g2, b2)


def kernel(x, c, w_mod, b_mod, w_in, conv_w, mu_shift, w0, w_decay_up, a0, a_up, g_up, k_k, k_a, r_k,
           lnx_g, lnx_b, w_out, ln1_g, ln1_b, w_ffn_in, w_ffn_out, ln2_g, ln2_b):
    depth = w_mod.shape[0]
    d = x.shape[-1]
    mod = _modulation(c, w_mod, b_mod)

    head = jnp.arange(RWKV_WIDTH) // HEAD_SIZE
    ones_bd = (head[:, None] == head[None, :]).astype(BF16)
    zeros_half = jnp.zeros((LORA_WA // 2, RWKV_WIDTH), F32)
    row2 = lambda v: v.reshape(1, -1)

    for l in range(depth):
        m = [mod[l, :, i * d:(i + 1) * d][:, None, :] for i in range(N_MOD)]
        wd = jnp.concatenate([w_decay_up[l], zeros_half], axis=0).astype(BF16)
        au = jnp.concatenate([zeros_half, a_up[l]], axis=0).astype(BF16)
        yc, r, lw, k, v, kkn, bv, g = _inproj(
            x, m[0], m[1], w_in[l].astype(BF16), conv_w[l], row2(mu_shift[l]), row2(w0[l]), row2(a0[l]),
            row2(k_k[l]), row2(k_a[l]), wd, au, g_up[l].astype(BF16), ones_bd)
        yr = _rwkv(r, lw, k, v, kkn, bv, g, row2(r_k[l]), row2(lnx_g[l]), row2(lnx_b[l]), ones_bd)
        x = _out_ffn(x, yc, yr, m[2], m[3], m[4], m[5], w_out[l].astype(BF16), row2(ln1_g[l]), row2(ln1_b[l]),
                     w_ffn_in[l].astype(BF16), w_ffn_out[l].astype(BF16), row2(ln2_g[l]), row2(ln2_b[l]))
    return x
```

```python
import jax
import jax.numpy as jnp
from jax import lax
from jax.experimental import pallas as pl
from jax.experimental.pallas import tpu as pltpu

F32 = jnp.float32
BF16 = jnp.bfloat16

D_MODEL = 1024
DEPTH = 4
CONV_WIDTH = 512
RWKV_WIDTH = 512
HEAD_SIZE = 64
N_HEADS = RWKV_WIDTH // HEAD_SIZE
LORA_WA = 128
LORA_G = 128
D_FF = 2816
N_MOD = 6
N_CONV_COLS = 3 * CONV_WIDTH
N_SHIFT_COLS = 3 * RWKV_WIDTH + LORA_WA + LORA_G
N_IN_COLS = N_CONV_COLS + N_SHIFT_COLS
ALPHA = (2.0 * DEPTH) ** 0.25
LN_EPS = 1e-5
GN_EPS = 64e-5

CHUNK = 64
GROUP = 256
HEADS_PER_GROUP = GROUP // HEAD_SIZE
N_GROUPS = RWKV_WIDTH // GROUP
SUBLANES = 8
HANDOFF = ("yconv", "r", "lw", "k", "v", "kkn", "bv", "g")

TS_IN = 1024
TS_RWKV = 256
TS_OUT = 1024
SUB_ROWS = 256
MXU_WIDTH = 256
FF_BLOCKS = (6 * MXU_WIDTH, 5 * MXU_WIDTH)
MOD_TN = 1536
VMEM_LIMIT = 56 * 1024 * 1024


def _nn(a, b):
    return lax.dot_general(a, b, (((1,), (0,)), ((), ())), preferred_element_type=F32)


def _nt(a, b):
    return lax.dot_general(a, b, (((1,), (1,)), ((), ())), preferred_element_type=F32)


def _tn(a, b):
    return lax.dot_general(a, b, (((0,), (0,)), ((), ())), preferred_element_type=F32)


def _sigmoid(x):
    return 1.0 / (1.0 + jnp.exp(-x))


def _softplus(x):
    return jnp.maximum(x, 0.0) + jnp.log(1.0 + jnp.exp(-jnp.abs(x)))


def _layer_norm(x, g, b):
    mu = jnp.mean(x, axis=-1, keepdims=True)
    d = x - mu
    var = jnp.mean(d * d, axis=-1, keepdims=True)
    return d * lax.rsqrt(var + LN_EPS) * g + b


def _interleave(*stages):
    done = [0] * len(stages)
    alive = [True] * len(stages)
    while any(alive):
        i = min((j for j in range(len(stages)) if alive[j]), key=lambda j: done[j] / stages[j][1])
        try:
            next(stages[i][0])
            done[i] += 1
        except StopIteration:
            alive[i] = False


def _mod_kernel(c_ref, w_ref, b_ref, o_ref):
    c = c_ref[...]
    s = c * _sigmoid(c)
    o_ref[0] = lax.dot_general(s, w_ref[0], (((1,), (0,)), ((), ())), precision=lax.Precision.HIGHEST,
                               preferred_element_type=F32) + b_ref[0]


def _modulation(c, w_mod, b_mod):
    bsz, d = c.shape
    depth, _, n = w_mod.shape
    rows = -(-bsz // SUBLANES) * SUBLANES
    c_pad = jnp.zeros((rows, d), F32).at[:bsz].set(c)
    out = pl.pallas_call(
        _mod_kernel,
        grid=(depth, n // MOD_TN),
        in_specs=[
            pl.BlockSpec((rows, d), lambda l, j: (0, 0)),
            pl.BlockSpec((1, d, MOD_TN), lambda l, j: (l, 0, j)),
            pl.BlockSpec((1, 1, MOD_TN), lambda l, j: (l, 0, j)),
        ],
        out_specs=pl.BlockSpec((1, rows, MOD_TN), lambda l, j: (l, 0, j)),
        out_shape=jax.ShapeDtypeStruct((depth, rows, n), F32),
        compiler_params=pltpu.CompilerParams(dimension_semantics=("arbitrary", "arbitrary"),
                                             vmem_limit_bytes=VMEM_LIMIT),
        name="modulation",
    )(c_pad, w_mod, b_mod.reshape(depth, 1, n))
    return out[:, :bsz]


def _inproj_kernel(x_ref, sh_ref, sc_ref, win_ref, convw_ref, mu_ref, w0_ref, a0_ref, kk_ref, ka_ref,
                   wd_ref, au_ref, gu_ref, ones_ref,
                   yconv_ref, r_ref, lw_ref, k_ref, v_ref, kkn_ref, bv_ref, g_ref,
                   pbuf, zbuf):
    ts = x_ref.shape[1]

    @pl.when(pl.program_id(1) == 0)
    def _():
        pbuf[0:SUBLANES, :] = jnp.zeros((SUBLANES, N_SHIFT_COLS), F32)
        zbuf[0:SUBLANES, :] = jnp.zeros((SUBLANES, CONV_WIDTH), F32)

    sub = min(SUB_ROWS, ts)
    nsub = ts // sub
    ps_ = {}
    w3 = 3 * RWKV_WIDTH

    def stage_project(i):
        rows = slice(i * sub, (i + 1) * sub)
        h = x_ref[0, rows, :] * (1.0 + sc_ref[0]) + sh_ref[0]
        ps_[i] = _nn(h.astype(BF16), win_ref[...])
        yield

    def stage_mixers(i):
        p = ps_[i]
        rows = slice(i * sub, (i + 1) * sub)
        lo = SUBLANES + i * sub
        bg = p[:, 0:CONV_WIDTH]
        z = p[:, CONV_WIDTH:2 * CONV_WIDTH] * p[:, 2 * CONV_WIDTH:3 * CONV_WIDTH]
        zbuf[lo:lo + sub, :] = z
        z1 = zbuf[lo - 1:lo - 1 + sub, :]
        z2 = zbuf[lo - 2:lo - 2 + sub, :]
        cw = convw_ref[...]
        yconv_ref[0, rows, :] = (bg * (cw[0:1] * z2 + cw[1:2] * z1 + cw[2:3] * z)).astype(BF16)
        yield

        ps_raw = p[:, N_CONV_COLS:N_IN_COLS]
        pbuf[lo:lo + sub, :] = ps_raw
        prev = pbuf[lo - 1:lo - 1 + sub, :]
        ps = ps_raw + mu_ref[...] * (prev - ps_raw)
        r = ps[:, 0:RWKV_WIDTH]
        k = ps[:, RWKV_WIDTH:2 * RWKV_WIDTH]
        wa = ps[:, w3:w3 + LORA_WA]
        gl = ps[:, w3 + LORA_WA:w3 + LORA_WA + LORA_G]
        r_ref[0, rows, :] = r.astype(BF16)
        v_ref[0, rows, :] = ps[:, 2 * RWKV_WIDTH:w3].astype(BF16)
        yield

        dw = _nn(jnp.tanh(wa).astype(BF16), wd_ref[...])
        da = _nn(wa.astype(BF16), au_ref[...])
        w_log = -_softplus(-(w0_ref[...] + dw)) - 0.5
        lw_ref[0, rows, :] = -jnp.exp(w_log)
        a = _sigmoid(a0_ref[...] + da)
        g_ref[0, rows, :] = _nn(_sigmoid(gl).astype(BF16), gu_ref[...]).astype(BF16)
        yield

        kkv = k * kk_ref[...]
        ss = _nn((kkv * kkv).astype(BF16), ones_ref[...])
        kkn = kkv * lax.rsqrt(jnp.maximum(ss, 1e-24))
        k_ref[0, rows, :] = (k * (1.0 + (a - 1.0) * ka_ref[...])).astype(BF16)
        kkn_ref[0, rows, :] = kkn.astype(BF16)
        bv_ref[0, rows, :] = (kkn * a).astype(BF16)
        yield

    _interleave((stage_project(0), 1))
    for i in range(nsub):
        work = [(stage_mixers(i), 4)]
        if i + 1 < nsub:
            work.append((stage_project(i + 1), 1))
        _interleave(*work)
    zbuf[0:SUBLANES, :] = zbuf[ts:ts + SUBLANES, :]
    pbuf[0:SUBLANES, :] = pbuf[ts:ts + SUBLANES, :]


def _inproj(x, sh1, sc1, win, convw, mu, w0, a0, k_k, k_a, wd, au, gu, ones_bd):
    bsz, seq, d = x.shape
    ts = min(TS_IN, seq)
    const = lambda shape: pl.BlockSpec(shape, lambda b, t: (0,) * len(shape), pipeline_mode=pl.Buffered(1))
    row = pl.BlockSpec((1, ts, RWKV_WIDTH), lambda b, t: (b, t, 0))
    out_sds = [jax.ShapeDtypeStruct((bsz, seq, RWKV_WIDTH), F32 if name == "lw" else BF16) for name in HANDOFF]
    return pl.pallas_call(
        _inproj_kernel,
        grid=(bsz, seq // ts),
        in_specs=[
            pl.BlockSpec((1, ts, d), lambda b, t: (b, t, 0)),
            pl.BlockSpec((1, 1, d), lambda b, t: (b, 0, 0)),
            pl.BlockSpec((1, 1, d), lambda b, t: (b, 0, 0)),
            const((d, N_IN_COLS)),
            const((3, CONV_WIDTH)),
            const((1, N_SHIFT_COLS)),
            const((1, RWKV_WIDTH)), const((1, RWKV_WIDTH)), const((1, RWKV_WIDTH)), const((1, RWKV_WIDTH)),
            const((LORA_WA, RWKV_WIDTH)), const((LORA_WA, RWKV_WIDTH)), const((LORA_G, RWKV_WIDTH)),
            const((RWKV_WIDTH, RWKV_WIDTH)),
        ],
        out_specs=[row] * 8,
        out_shape=out_sds,
        scratch_shapes=[pltpu.VMEM((ts + SUBLANES, N_SHIFT_COLS), F32),
                        pltpu.VMEM((ts + SUBLANES, CONV_WIDTH), F32)],
        compiler_params=pltpu.CompilerParams(dimension_semantics=("arbitrary", "arbitrary"),
                                             vmem_limit_bytes=VMEM_LIMIT),
        name="inproj",
    )(x, sh1, sc1, win, convw, mu, w0, a0, k_k, k_a, wd, au, gu, ones_bd)


def _rwkv_kernel(r_ref, lw_ref, k_ref, v_ref, kkn_ref, bv_ref, g_ref, rk_ref, lng_ref, lnb_ref, ones_ref,
                 y_ref, s_ref):
    nb, ts, _ = r_ref.shape
    c = CHUNK
    nck = ts // c

    @pl.when(pl.program_id(0) == 0)
    def _():
        s_ref[...] = jnp.zeros(s_ref.shape, F32)

    rr = lax.broadcasted_iota(jnp.int32, (GROUP, GROUP), 0) // HEAD_SIZE
    cc = lax.broadcasted_iota(jnp.int32, (GROUP, GROUP), 1) // HEAD_SIZE
    bd = rr == cc
    bd_bf = bd.astype(BF16)
    ti = lax.broadcasted_iota(jnp.int32, (c, GROUP), 0)
    tj = lax.broadcasted_iota(jnp.int32, (c, GROUP), 1) % c
    strict = ti > tj
    incl = ti >= tj
    eye = (ti == tj).astype(F32)

    def level_mask(s):
        return (ti // (2 * s) == tj // (2 * s)) & (ti % (2 * s) >= s) & (tj % (2 * s) < s)

    trow = lax.broadcasted_iota(jnp.int32, (c, RWKV_WIDTH), 0)

    def vt(x):
        xb = x.astype(BF16)
        return jnp.concatenate([xb] * HEADS_PER_GROUP, axis=0) * bd_bf

    lanes = lambda gi: slice(gi * GROUP, (gi + 1) * GROUP)

    prep, st, ys = {}, {}, {}
    seqs = [(b, gi) for b in range(nb) for gi in range(N_GROUPS)]
    state = {sq: s_ref[sq[0], sq[1]] for sq in seqs}


    def stage_operands(ci):
        for b in range(nb):
            rows = slice(ci * c, (ci + 1) * c)
            lw = lw_ref[b, rows, :]
            k = k_ref[b, rows, :].astype(F32)
            kkn = kkn_ref[b, rows, :].astype(F32)
            bv = bv_ref[b, rows, :].astype(F32)
            cl = lw
            dist = 1
            while dist < c:
                cl = cl + jnp.where(trow >= dist, pltpu.roll(cl, dist, axis=0), 0.0)
                dist *= 2
            cl_end = cl[c - 1:c, :]
            e_neg = jnp.exp(-cl)
            e_end = jnp.exp(cl_end - cl)
            p = dict(
                rt=(r_ref[b, rows, :].astype(F32) * jnp.exp(cl)).astype(BF16),
                at=(-kkn * jnp.exp(cl - lw)).astype(BF16),
                kt=(k * e_neg).astype(BF16),
                bt=(bv * e_neg).astype(BF16),
                kh=(k * e_end).astype(BF16),
                bh=(bv * e_end).astype(BF16),
                v=v_ref[b, rows, :],
                w_end=jnp.exp(cl_end))
            prep[(b, ci)] = p
            yield
            for gi in range(N_GROUPS):
                ln = lanes(gi)
                ar = jnp.concatenate([p["at"][:, ln], p["rt"][:, ln]], axis=0)
                pb = _nt(ar, vt(p["bt"][:, ln]))
                pk = _nt(ar, vt(p["kt"][:, ln]))
                a_ab = jnp.where(strict, pb[0:c], 0.0)
                st[(b, ci, gi)] = dict(
                    a_ab=a_ab,
                    a_ak=jnp.where(strict, pk[0:c], 0.0).astype(BF16),
                    m_rb=jnp.where(incl, pb[c:2 * c], 0.0).astype(BF16),
                    m_rk=jnp.where(incl, pk[c:2 * c], 0.0).astype(BF16),
                    vt_v=vt(p["v"][:, ln]),
                    t_inv=eye + jnp.where(level_mask(1), a_ab, 0.0))
                yield

    def stage_solve(ci):
        chs = [(b, ci, gi) for (b, gi) in seqs]
        s = 2
        while s < c:
            lm = level_mask(s)
            xs = {}
            for ch in chs:
                xs[ch] = _nn(jnp.where(lm, st[ch]["a_ab"], 0.0).astype(BF16), vt(st[ch]["t_inv"]))
                yield
            for ch in chs:
                st[ch]["t_inv"] = st[ch]["t_inv"] + _nn(st[ch]["t_inv"].astype(BF16), vt(xs[ch]))
                yield
            s *= 2
        for ch in chs:
            d = st[ch]
            avy = _nn(jnp.concatenate([d["a_ak"], d["m_rk"]], axis=0), d["vt_v"])
            d["av"] = avy[0:c]
            d["yloc"] = avy[c:2 * c]
            yield
        for ch in chs:
            b, _, gi = ch
            d = st[ch]
            tb = d["t_inv"].astype(BF16)
            d["om"] = _nn(tb, vt(prep[(b, ci)]["at"][:, lanes(gi)])).astype(BF16)
            yield
            d["u"] = _nn(tb, vt(d["av"]))
            yield

    def stage_state(ci):
        res, sa, upd = {}, {}, {}
        for (b, gi) in seqs:
            lhs = jnp.concatenate([st[(b, ci, gi)]["om"], prep[(b, ci)]["rt"][:, lanes(gi)]], axis=0)
            res[(b, gi)] = _nt(lhs, state[(b, gi)].astype(BF16))
            yield
        for sq in seqs:
            sa[sq] = res[sq][0:c] + st[(sq[0], ci, sq[1])]["u"]
        for (b, gi) in seqs:
            p = prep[(b, ci)]
            ln = lanes(gi)
            lhs = jnp.concatenate([sa[(b, gi)].astype(BF16), p["v"][:, ln]], axis=0)
            rhs = jnp.concatenate([p["bh"][:, ln], p["kh"][:, ln]], axis=0)
            upd[(b, gi)] = _tn(lhs, rhs)
            yield
        for (b, gi) in seqs:
            d = st[(b, ci, gi)]
            ys[(b, ci, gi)] = res[(b, gi)][c:2 * c] + _nn(d["m_rb"], vt(sa[(b, gi)])) + d["yloc"]
            state[(b, gi)] = (state[(b, gi)] * prep[(b, ci)]["w_end"][:, lanes(gi)]
                              + jnp.where(bd, upd[(b, gi)], 0.0))
            yield

    n_seq = len(seqs)
    n_levels = 0
    s = 2
    while s < c:
        n_levels += 1
        s *= 2
    len_operands = nb * (1 + N_GROUPS)
    len_solve = n_seq * (2 * n_levels + 3)
    len_state = 3 * n_seq

    ones = ones_ref[...]
    inv_n = 1.0 / HEAD_SIZE

    def stage_output(ci):
        rows = slice(ci * c, (ci + 1) * c)
        y = jnp.concatenate([jnp.concatenate([ys[(b, ci, gi)] for gi in range(N_GROUPS)], axis=1)
                             for b in range(nb)], axis=0)
        allrows = lambda ref: jnp.concatenate([ref[b, rows, :] for b in range(nb)], axis=0).astype(F32)
        mu = _nn(y.astype(BF16), ones) * inv_n
        yield
        dlt = y - mu
        var = _nn((dlt * dlt).astype(BF16), ones) * inv_n
        yield
        yn = dlt * lax.rsqrt(var + GN_EPS) * lng_ref[...] + lnb_ref[...]
        bonus = _nn((allrows(r_ref) * allrows(k_ref) * rk_ref[...]).astype(BF16), ones) * allrows(v_ref)
        yield
        out = (yn + bonus) * allrows(g_ref)
        for b in range(nb):
            y_ref[b, rows, :] = out[b * c:(b + 1) * c].astype(BF16)
        yield

    len_output = 4

    _interleave((stage_operands(0), len_operands))
    for ci in range(nck):
        work = [(stage_solve(ci), len_solve)]
        if ci + 1 < nck:
            work.append((stage_operands(ci + 1), len_operands))
        if ci > 0:
            work.append((stage_state(ci - 1), len_state))
        if ci > 1:
            work.append((stage_output(ci - 2), len_output))
        _interleave(*work)
    work = [(stage_state(nck - 1), len_state)]
    if nck > 1:
        work.append((stage_output(nck - 2), len_output))
    _interleave(*work)
    _interleave((stage_output(nck - 1), len_output))
    for (b, gi) in seqs:
        s_ref[b, gi] = state[(b, gi)]


def _rwkv(r, lw, k, v, kkn, bv, g, r_k, lnx_g, lnx_b, ones_bd):
    bsz, seq, w = r.shape
    ts = min(TS_RWKV, seq)
    row = pl.BlockSpec((bsz, ts, w), lambda t: (0, t, 0))
    const = lambda shape: pl.BlockSpec(shape, lambda t: (0,) * len(shape), pipeline_mode=pl.Buffered(1))
    return pl.pallas_call(
        _rwkv_kernel,
        grid=(seq // ts,),
        in_specs=[row] * 7 + [const((1, w))] * 3 + [const((w, w))],
        out_specs=row,
        out_shape=jax.ShapeDtypeStruct((bsz, seq, w), BF16),
        scratch_shapes=[pltpu.VMEM((bsz, N_GROUPS, GROUP, GROUP), F32)],
        compiler_params=pltpu.CompilerParams(dimension_semantics=("arbitrary",),
                                             vmem_limit_bytes=VMEM_LIMIT),
        name="rwkv7_scan",
    )(r, lw, k, v, kkn, bv, g, r_k, lnx_g, lnx_b, ones_bd)


def _out_ffn_kernel(x_ref, yc_ref, yr_ref, gt1_ref, sh2_ref, sc2_ref, gt2_ref, wout_ref, g1_ref, b1_ref,
                    wfi_ref, wfo_ref, g2_ref, b2_ref, o_ref):
    ts = x_ref.shape[1]
    sub = min(SUB_ROWS, ts)
    nsub = ts // sub
    x1s, hbs, fs = {}, {}, {}

    def stage_mix(i):
        rows = slice(i * sub, (i + 1) * sub)
        mix = (_nn(yc_ref[0, rows, :], wout_ref[0:CONV_WIDTH, :])
               + _nn(yr_ref[0, rows, :], wout_ref[CONV_WIDTH:CONV_WIDTH + RWKV_WIDTH, :]))
        yield
        x1 = _layer_norm(ALPHA * x_ref[0, rows, :] + (1.0 + gt1_ref[0]) * mix, g1_ref[...], b1_ref[...])
        x1s[i] = x1
        hbs[i] = (x1 * (1.0 + sc2_ref[0]) + sh2_ref[0]).astype(BF16)
        yield

    def stage_ffn(i):
        acc = None
        j = 0
        for blk in FF_BLOCKS:
            gate = _nn(hbs[i], wfi_ref[:, j:j + blk])
            up = _nn(hbs[i], wfi_ref[:, D_FF + j:D_FF + j + blk])
            yield
            part = _nn((gate * _sigmoid(gate) * up).astype(BF16), wfo_ref[j:j + blk, :])
            acc = part if acc is None else acc + part
            j += blk
            yield
        fs[i] = acc

    def stage_out(i):
        rows = slice(i * sub, (i + 1) * sub)
        o_ref[0, rows, :] = _layer_norm(ALPHA * x1s[i] + (1.0 + gt2_ref[0]) * fs[i], g2_ref[...], b2_ref[...])
        yield

    len_ffn = 2 * len(FF_BLOCKS)
    _interleave((stage_mix(0), 2))
    for i in range(nsub):
        work = [(stage_ffn(i), len_ffn)]
        if i + 1 < nsub:
            work.append((stage_mix(i + 1), 2))
        if i > 0:
            work.append((stage_out(i - 1), 1))
        _interleave(*work)
    _interleave((stage_out(nsub - 1), 1))


def _out_ffn(x, yc, yr, gt1, sh2, sc2, gt2, wout, g1, b1, wfi, wfo, g2, b2):
    bsz, seq, d = x.shape
    ts = min(TS_OUT, seq)
    const = lambda shape: pl.BlockSpec(shape, lambda b, t: (0,) * len(shape), pipeline_mode=pl.Buffered(1))
    modv = pl.BlockSpec((1, 1, d), lambda b, t: (b, 0, 0))
    return pl.pallas_call(
        _out_ffn_kernel,
        grid=(bsz, seq // ts),
        in_specs=[
            pl.BlockSpec((1, ts, d), lambda b, t: (b, t, 0)),
            pl.BlockSpec((1, ts, CONV_WIDTH), lambda b, t: (b, t, 0)),
            pl.BlockSpec((1, ts, RWKV_WIDTH), lambda b, t: (b, t, 0)),
            modv, modv, modv, modv,
            const((d, d)), const((1, d)), const((1, d)),
            const((d, 2 * D_FF)), const((D_FF, d)), const((1, d)), const((1, d)),
        ],
        out_specs=pl.BlockSpec((1, ts, d), lambda b, t: (b, t, 0)),
        out_shape=jax.ShapeDtypeStruct((bsz, seq, d), F32),
        compiler_params=pltpu.CompilerParams(dimension_semantics=("parallel", "parallel"),
                                             vmem_limit_bytes=VMEM_LIMIT),
        name="out_ffn",
    )(x, yc, yr, gt1, sh2, sc2, gt2, wout, g1, b1, wfi, wfo, g2, b2)


def kernel(x, c, w_mod, b_mod, w_in, conv_w, mu_shift, w0, w_decay_up, a0, a_up, g_up, k_k, k_a, r_k,
           lnx_g, lnx_b, w_out, ln1_g, ln1_b, w_ffn_in, w_ffn_out, ln2_g, ln2_b):
    depth = w_mod.shape[0]
    d = x.shape[-1]
    mod = _modulation(c, w_mod, b_mod)

    head = jnp.arange(RWKV_WIDTH) // HEAD_SIZE
    ones_bd = (head[:, None] == head[None, :]).astype(BF16)
    zeros_half = jnp.zeros((LORA_WA // 2, RWKV_WIDTH), F32)
    row2 = lambda v: v.reshape(1, -1)

    for l in range(depth):
        m = [mod[l, :, i * d:(i + 1) * d][:, None, :] for i in range(N_MOD)]
        wd = jnp.concatenate([w_decay_up[l], zeros_half], axis=0).astype(BF16)
        au = jnp.concatenate([zeros_half, a_up[l]], axis=0).astype(BF16)
        yc, r, lw, k, v, kkn, bv, g = _inproj(
            x, m[0], m[1], w_in[l].astype(BF16), conv_w[l], row2(mu_shift[l]), row2(w0[l]), row2(a0[l]),
            row2(k_k[l]), row2(k_a[l]), wd, au, g_up[l].astype(BF16), ones_bd)
        yr = _rwkv(r, lw, k, v, kkn, bv, g, row2(r_k[l]), row2(lnx_g[l]), row2(lnx_b[l]), ones_bd)
        x = _out_ffn(x, yc, yr, m[2], m[3], m[4], m[5], w_out[l].astype(BF16), row2(ln1_g[l]), row2(ln1_b[l]),
                     w_ffn_in[l].astype(BF16), w_ffn_out[l].astype(BF16), row2(ln2_g[l]), row2(ln2_b[l]))
    return x
```

```python
import jax
import jax.numpy as jnp
from jax import lax
from jax.experimental import pallas as pl
from jax.experimental.pallas import tpu as pltpu

F32 = jnp.float32
BF16 = jnp.bfloat16

D_MODEL = 1024
DEPTH = 4
CONV_WIDTH = 512
RWKV_WIDTH = 512
HEAD_SIZE = 64
N_HEADS = RWKV_WIDTH // HEAD_SIZE
LORA_WA = 128
LORA_G = 128
D_FF = 2816
N_MOD = 6
MOD_SHIFT1, MOD_SCALE1, MOD_GATE1, MOD_SHIFT2, MOD_SCALE2, MOD_GATE2 = range(N_MOD)
N_CONV_COLS = 3 * CONV_WIDTH
N_SHIFT_COLS = 3 * RWKV_WIDTH + LORA_WA + LORA_G
N_IN_COLS = N_CONV_COLS + N_SHIFT_COLS
ALPHA = (2.0 * DEPTH) ** 0.25
LN_EPS = 1e-5
GN_EPS = 64e-5

CHUNK = 64
GROUP = 256
HEADS_PER_GROUP = GROUP // HEAD_SIZE
N_GROUPS = RWKV_WIDTH // GROUP
SUBLANES = 8

TS_IN = 512
TS_RWKV = 256
TS_OUT = 1024
SUB_ROWS = 256
MXU_WIDTH = 256
FF_BLOCKS = (6 * MXU_WIDTH, 5 * MXU_WIDTH)
MOD_TN = 1536
VMEM_LIMIT = 56 * 1024 * 1024


def _nn(a, b):
    return lax.dot_general(a, b, (((1,), (0,)), ((), ())), preferred_element_type=F32)


def _nt(a, b):
    return lax.dot_general(a, b, (((1,), (1,)), ((), ())), preferred_element_type=F32)


def _tn(a, b):
    return lax.dot_general(a, b, (((0,), (0,)), ((), ())), preferred_element_type=F32)


def _sigmoid(x):
    return 1.0 / (1.0 + jnp.exp(-x))


def _softplus(x):
    return jnp.maximum(x, 0.0) + jnp.log(1.0 + jnp.exp(-jnp.abs(x)))


def _layer_norm(x, g, b):
    mu = jnp.mean(x, axis=-1, keepdims=True)
    d = x - mu
    var = jnp.mean(d * d, axis=-1, keepdims=True)
    return d * lax.rsqrt(var + LN_EPS) * g + b


def _interleave(*stages):
    done = [0] * len(stages)
    alive = [True] * len(stages)
    while any(alive):
        i = min((j for j in range(len(stages)) if alive[j]), key=lambda j: done[j] / stages[j][1])
        try:
            next(stages[i][0])
            done[i] += 1
        except StopIteration:
            alive[i] = False


def _mod_kernel(c_ref, w_ref, b_ref, o_ref):
    c = c_ref[...]
    s = c * _sigmoid(c)
    o_ref[0] = lax.dot_general(s, w_ref[0], (((1,), (0,)), ((), ())), precision=lax.Precision.HIGHEST,
                               preferred_element_type=F32) + b_ref[0]


def _modulation(c, w_mod, b_mod):
    bsz, d = c.shape
    depth, _, n = w_mod.shape
    rows = -(-bsz // SUBLANES) * SUBLANES
    c_pad = jnp.zeros((rows, d), F32).at[:bsz].set(c)
    out = pl.pallas_call(
        _mod_kernel,
        grid=(depth, n // MOD_TN),
        in_specs=[
            pl.BlockSpec((rows, d), lambda l, j: (0, 0)),
            pl.BlockSpec((1, d, MOD_TN), lambda l, j: (l, 0, j)),
            pl.BlockSpec((1, 1, MOD_TN), lambda l, j: (l, 0, j)),
        ],
        out_specs=pl.BlockSpec((1, rows, MOD_TN), lambda l, j: (l, 0, j)),
        out_shape=jax.ShapeDtypeStruct((depth, rows, n), F32),
        compiler_params=pltpu.CompilerParams(dimension_semantics=("arbitrary", "arbitrary"),
                                             vmem_limit_bytes=VMEM_LIMIT),
        name="modulation",
    )(c_pad, w_mod, b_mod.reshape(depth, 1, n))
    return out[:, :bsz]


def _inproj_kernel(x_ref, sh_ref, sc_ref, win_ref, convw_ref, mu_ref, w0_ref, a0_ref, kk_ref, ka_ref,
                   wd_ref, au_ref, gu_ref, ones_ref,
                   yconv_ref, r_ref, lw_ref, k_ref, v_ref, kkn_ref, bv_ref, g_ref,
                   pbuf, zbuf):
    ts = x_ref.shape[1]

    @pl.when(pl.program_id(1) == 0)
    def _():
        pbuf[0:SUBLANES, :] = jnp.zeros((SUBLANES, N_SHIFT_COLS), F32)
        zbuf[0:SUBLANES, :] = jnp.zeros((SUBLANES, CONV_WIDTH), F32)

    sub = min(SUB_ROWS, ts)
    nsub = ts // sub
    ps_ = {}
    w3 = 3 * RWKV_WIDTH

    def stage_project(i):
        rows = slice(i * sub, (i + 1) * sub)
        h = x_ref[0, rows, :] * (1.0 + sc_ref[...]) + sh_ref[...]
        ps_[i] = _nn(h.astype(BF16), win_ref[...])
        yield

    def stage_mixers(i):
        p = ps_[i]
        rows = slice(i * sub, (i + 1) * sub)
        lo = SUBLANES + i * sub
        bg = p[:, 0:CONV_WIDTH]
        z = p[:, CONV_WIDTH:2 * CONV_WIDTH] * p[:, 2 * CONV_WIDTH:3 * CONV_WIDTH]
        zbuf[lo:lo + sub, :] = z
        z1 = zbuf[lo - 1:lo - 1 + sub, :]
        z2 = zbuf[lo - 2:lo - 2 + sub, :]
        cw = convw_ref[...]
        yconv_ref[0, rows, :] = bg * (cw[0:1] * z2 + cw[1:2] * z1 + cw[2:3] * z)
        yield

        ps_raw = p[:, N_CONV_COLS:N_IN_COLS]
        pbuf[lo:lo + sub, :] = ps_raw
        prev = pbuf[lo - 1:lo - 1 + sub, :]
        ps = ps_raw + mu_ref[...] * (prev - ps_raw)
        r = ps[:, 0:RWKV_WIDTH]
        k = ps[:, RWKV_WIDTH:2 * RWKV_WIDTH]
        wa = ps[:, w3:w3 + LORA_WA]
        gl = ps[:, w3 + LORA_WA:w3 + LORA_WA + LORA_G]
        r_ref[0, rows, :] = r
        v_ref[0, rows, :] = ps[:, 2 * RWKV_WIDTH:w3]
        yield

        dw = _nn(jnp.tanh(wa).astype(BF16), wd_ref[...])
        da = _nn(wa.astype(BF16), au_ref[...])
        w_log = -_softplus(-(w0_ref[...] + dw)) - 0.5
        lw_ref[0, rows, :] = -jnp.exp(w_log)
        a = _sigmoid(a0_ref[...] + da)
        g_ref[0, rows, :] = _nn(_sigmoid(gl).astype(BF16), gu_ref[...])
        yield

        kkv = k * kk_ref[...]
        ss = _nn((kkv * kkv).astype(BF16), ones_ref[...])
        kkn = kkv * lax.rsqrt(jnp.maximum(ss, 1e-24))
        k_ref[0, rows, :] = k * (1.0 + (a - 1.0) * ka_ref[...])
        kkn_ref[0, rows, :] = kkn
        bv_ref[0, rows, :] = kkn * a
        yield

    _interleave((stage_project(0), 1))
    for i in range(nsub):
        work = [(stage_mixers(i), 4)]
        if i + 1 < nsub:
            work.append((stage_project(i + 1), 1))
        _interleave(*work)
    zbuf[0:SUBLANES, :] = zbuf[ts:ts + SUBLANES, :]
    pbuf[0:SUBLANES, :] = pbuf[ts:ts + SUBLANES, :]


def _inproj(layer, x, mod, win, convw, mu, w0, a0, k_k, k_a, wd, au, gu, ones_bd):
    bsz, seq, d = x.shape
    ts = min(TS_IN, seq)
    const = lambda shape: pl.BlockSpec(shape, lambda b, t: (0,) * len(shape), pipeline_mode=pl.Buffered(1))
    lay = lambda shape: pl.BlockSpec((None,) + shape, lambda b, t: (layer,) + (0,) * len(shape),
                                     pipeline_mode=pl.Buffered(1))
    modv = lambda i: pl.BlockSpec((None, None, None, 1, d), lambda b, t: (layer, b, i, 0, 0))
    row = pl.BlockSpec((1, ts, RWKV_WIDTH), lambda b, t: (b, t, 0))
    out_sds = jax.ShapeDtypeStruct((bsz, seq, RWKV_WIDTH), F32)
    return pl.pallas_call(
        _inproj_kernel,
        grid=(bsz, seq // ts),
        in_specs=[
            pl.BlockSpec((1, ts, d), lambda b, t: (b, t, 0)),
            modv(MOD_SHIFT1), modv(MOD_SCALE1),
            lay((d, N_IN_COLS)),
            lay((3, CONV_WIDTH)),
            lay((1, N_SHIFT_COLS)),
            lay((1, RWKV_WIDTH)), lay((1, RWKV_WIDTH)), lay((1, RWKV_WIDTH)), lay((1, RWKV_WIDTH)),
            lay((LORA_WA, RWKV_WIDTH)), lay((LORA_WA, RWKV_WIDTH)), lay((LORA_G, RWKV_WIDTH)),
            const((RWKV_WIDTH, RWKV_WIDTH)),
        ],
        out_specs=[row] * 8,
        out_shape=[out_sds] * 8,
        scratch_shapes=[pltpu.VMEM((ts + SUBLANES, N_SHIFT_COLS), F32),
                        pltpu.VMEM((ts + SUBLANES, CONV_WIDTH), F32)],
        compiler_params=pltpu.CompilerParams(dimension_semantics=("arbitrary", "arbitrary"),
                                             vmem_limit_bytes=VMEM_LIMIT),
        name="inproj",
    )(x, mod, mod, win, convw, mu, w0, a0, k_k, k_a, wd, au, gu, ones_bd)


def _rwkv_kernel(r_ref, lw_ref, k_ref, v_ref, kkn_ref, bv_ref, g_ref, rk_ref, lng_ref, lnb_ref, ones_ref,
                 y_ref, s_ref):
    nb, ts, _ = r_ref.shape
    c = CHUNK
    nck = ts // c

    @pl.when(pl.program_id(0) == 0)
    def _():
        s_ref[...] = jnp.zeros(s_ref.shape, F32)

    rr = lax.broadcasted_iota(jnp.int32, (GROUP, GROUP), 0) // HEAD_SIZE
    cc = lax.broadcasted_iota(jnp.int32, (GROUP, GROUP), 1) // HEAD_SIZE
    bd = rr == cc
    bd_bf = bd.astype(BF16)
    ti = lax.broadcasted_iota(jnp.int32, (c, GROUP), 0)
    tj = lax.broadcasted_iota(jnp.int32, (c, GROUP), 1) % c
    strict = ti > tj
    incl = ti >= tj
    eye = (ti == tj).astype(F32)

    def level_mask(s):
        return (ti // (2 * s) == tj // (2 * s)) & (ti % (2 * s) >= s) & (tj % (2 * s) < s)

    trow = lax.broadcasted_iota(jnp.int32, (c, RWKV_WIDTH), 0)

    def vt(x):
        xb = x.astype(BF16)
        return jnp.concatenate([xb] * HEADS_PER_GROUP, axis=0) * bd_bf

    lanes = lambda gi: slice(gi * GROUP, (gi + 1) * GROUP)

    prep, st, ys = {}, {}, {}
    seqs = [(b, gi) for b in range(nb) for gi in range(N_GROUPS)]
    state = {sq: s_ref[sq[0], sq[1]] for sq in seqs}


    def stage_operands(ci):
        for b in range(nb):
            rows = slice(ci * c, (ci + 1) * c)
            lw = lw_ref[b, rows, :]
            k = k_ref[b, rows, :]
            kkn = kkn_ref[b, rows, :]
            bv = bv_ref[b, rows, :]
            cl = lw
            dist = 1
            while dist < c:
                cl = cl + jnp.where(trow >= dist, pltpu.roll(cl, dist, axis=0), 0.0)
                dist *= 2
            cl_end = cl[c - 1:c, :]
            e_neg = jnp.exp(-cl)
            e_end = jnp.exp(cl_end - cl)
            p = dict(
                rt=(r_ref[b, rows, :] * jnp.exp(cl)).astype(BF16),
                at=(-kkn * jnp.exp(cl - lw)).astype(BF16),
                kt=(k * e_neg).astype(BF16),
                bt=(bv * e_neg).astype(BF16),
                kh=(k * e_end).astype(BF16),
                bh=(bv * e_end).astype(BF16),
                v=v_ref[b, rows, :].astype(BF16),
                w_end=jnp.exp(cl_end))
            prep[(b, ci)] = p
            yield
            for gi in range(N_GROUPS):
                ln = lanes(gi)
                ar = jnp.concatenate([p["at"][:, ln], p["rt"][:, ln]], axis=0)
                pb = _nt(ar, vt(p["bt"][:, ln]))
                pk = _nt(ar, vt(p["kt"][:, ln]))
                a_ab = jnp.where(strict, pb[0:c], 0.0)
                st[(b, ci, gi)] = dict(
                    a_ab=a_ab,
                    a_ak=jnp.where(strict, pk[0:c], 0.0).astype(BF16),
                    m_rb=jnp.where(incl, pb[c:2 * c], 0.0).astype(BF16),
                    m_rk=jnp.where(incl, pk[c:2 * c], 0.0).astype(BF16),
                    vt_v=vt(p["v"][:, ln]),
                    t_inv=eye + jnp.where(level_mask(1), a_ab, 0.0))
                yield

    def stage_solve(ci):
        chs = [(b, ci, gi) for (b, gi) in seqs]
        s = 2
        while s < c:
            lm = level_mask(s)
            xs = {}
            for ch in chs:
                xs[ch] = _nn(jnp.where(lm, st[ch]["a_ab"], 0.0).astype(BF16), vt(st[ch]["t_inv"]))
                yield
            for ch in chs:
                st[ch]["t_inv"] = st[ch]["t_inv"] + _nn(st[ch]["t_inv"].astype(BF16), vt(xs[ch]))
                yield
            s *= 2
        for ch in chs:
            d = st[ch]
            avy = _nn(jnp.concatenate([d["a_ak"], d["m_rk"]], axis=0), d["vt_v"])
            d["av"] = avy[0:c]
            d["yloc"] = avy[c:2 * c]
            d["tb"] = d["t_inv"].astype(BF16)
            yield

    def stage_state(ci):
        res, sa, upd = {}, {}, {}
        for (b, gi) in seqs:
            p = prep[(b, ci)]
            lhs = jnp.concatenate([p["at"][:, lanes(gi)], p["rt"][:, lanes(gi)]], axis=0)
            res[(b, gi)] = _nt(lhs, state[(b, gi)].astype(BF16))
            yield
        for (b, gi) in seqs:
            d = st[(b, ci, gi)]
            sa[(b, gi)] = _nn(d["tb"], vt(res[(b, gi)][0:c] + d["av"]))
            yield
        for (b, gi) in seqs:
            p = prep[(b, ci)]
            ln = lanes(gi)
            lhs = jnp.concatenate([sa[(b, gi)].astype(BF16), p["v"][:, ln]], axis=0)
            rhs = jnp.concatenate([p["bh"][:, ln], p["kh"][:, ln]], axis=0)
            upd[(b, gi)] = _tn(lhs, rhs)
            yield
        for (b, gi) in seqs:
            d = st[(b, ci, gi)]
            ys[(b, ci, gi)] = res[(b, gi)][c:2 * c] + _nn(d["m_rb"], vt(sa[(b, gi)])) + d["yloc"]
            state[(b, gi)] = (state[(b, gi)] * prep[(b, ci)]["w_end"][:, lanes(gi)]
                              + jnp.where(bd, upd[(b, gi)], 0.0))
            yield

    n_seq = len(seqs)
    n_levels = 0
    s = 2
    while s < c:
        n_levels += 1
        s *= 2
    len_operands = nb * (1 + N_GROUPS)
    len_solve = n_seq * (2 * n_levels + 1)
    len_state = 4 * n_seq

    ones = ones_ref[...]
    inv_n = 1.0 / HEAD_SIZE

    def stage_output(ci):
        rows = slice(ci * c, (ci + 1) * c)
        y = jnp.concatenate([jnp.concatenate([ys[(b, ci, gi)] for gi in range(N_GROUPS)], axis=1)
                             for b in range(nb)], axis=0)
        allrows = lambda ref: jnp.concatenate([ref[b, rows, :] for b in range(nb)], axis=0)
        mu = _nn(y.astype(BF16), ones) * inv_n
        yield
        dlt = y - mu
        var = _nn((dlt * dlt).astype(BF16), ones) * inv_n
        yield
        yn = dlt * lax.rsqrt(var + GN_EPS) * lng_ref[...] + lnb_ref[...]
        bonus = _nn((allrows(r_ref) * allrows(k_ref) * rk_ref[...]).astype(BF16), ones) * allrows(v_ref)
        yield
        out = (yn + bonus) * allrows(g_ref)
        for b in range(nb):
            y_ref[b, rows, :] = out[b * c:(b + 1) * c]
        yield

    len_output = 4

    _interleave((stage_operands(0), len_operands))
    for ci in range(nck):
        work = [(stage_solve(ci), len_solve)]
        if ci + 1 < nck:
            work.append((stage_operands(ci + 1), len_operands))
        if ci > 0:
            work.append((stage_state(ci - 1), len_state))
        if ci > 1:
            work.append((stage_output(ci - 2), len_output))
        _interleave(*work)
    work = [(stage_state(nck - 1), len_state)]
    if nck > 1:
        work.append((stage_output(nck - 2), len_output))
    _interleave(*work)
    _interleave((stage_output(nck - 1), len_output))
    for (b, gi) in seqs:
        s_ref[b, gi] = state[(b, gi)]


def _rwkv(layer, r, lw, k, v, kkn, bv, g, r_k, lnx_g, lnx_b, ones_bd):
    bsz, seq, w = r.shape
    ts = min(TS_RWKV, seq)
    row = pl.BlockSpec((bsz, ts, w), lambda t: (0, t, 0))
    const = lambda shape: pl.BlockSpec(shape, lambda t: (0,) * len(shape), pipeline_mode=pl.Buffered(1))
    lay = lambda shape: pl.BlockSpec((None,) + shape, lambda t: (layer,) + (0,) * len(shape),
                                     pipeline_mode=pl.Buffered(1))
    return pl.pallas_call(
        _rwkv_kernel,
        grid=(seq // ts,),
        in_specs=[row] * 7 + [lay((1, w))] * 3 + [const((w, w))],
        out_specs=row,
        out_shape=jax.ShapeDtypeStruct((bsz, seq, w), F32),
        scratch_shapes=[pltpu.VMEM((bsz, N_GROUPS, GROUP, GROUP), F32)],
        compiler_params=pltpu.CompilerParams(dimension_semantics=("arbitrary",),
                                             vmem_limit_bytes=VMEM_LIMIT),
        name="rwkv7_scan",
    )(r, lw, k, v, kkn, bv, g, r_k, lnx_g, lnx_b, ones_bd)


def _out_ffn_kernel(x_ref, yc_ref, yr_ref, gt1_ref, sh2_ref, sc2_ref, gt2_ref, wout_ref, g1_ref, b1_ref,
                    wfi_ref, wfo_ref, g2_ref, b2_ref, o_ref):
    ts = x_ref.shape[1]
    sub = min(SUB_ROWS, ts)
    nsub = ts // sub
    x1s, hbs, fs = {}, {}, {}

    def stage_mix(i):
        rows = slice(i * sub, (i + 1) * sub)
        mix = (_nn(yc_ref[0, rows, :].astype(BF16), wout_ref[0:CONV_WIDTH, :])
               + _nn(yr_ref[0, rows, :].astype(BF16), wout_ref[CONV_WIDTH:CONV_WIDTH + RWKV_WIDTH, :]))
        yield
        x1 = _layer_norm(ALPHA * x_ref[0, rows, :] + (1.0 + gt1_ref[...]) * mix, g1_ref[...], b1_ref[...])
        x1s[i] = x1
        hbs[i] = (x1 * (1.0 + sc2_ref[...]) + sh2_ref[...]).astype(BF16)
        yield

    def stage_ffn(i):
        acc = None
        j = 0
        for blk in FF_BLOCKS:
            gate = _nn(hbs[i], wfi_ref[:, j:j + blk])
            up = _nn(hbs[i], wfi_ref[:, D_FF + j:D_FF + j + blk])
            yield
            part = _nn((gate * _sigmoid(gate) * up).astype(BF16), wfo_ref[j:j + blk, :])
            acc = part if acc is None else acc + part
            j += blk
            yield
        fs[i] = acc

    def stage_out(i):
        rows = slice(i * sub, (i + 1) * sub)
        o_ref[0, rows, :] = _layer_norm(ALPHA * x1s[i] + (1.0 + gt2_ref[...]) * fs[i], g2_ref[...], b2_ref[...])
        yield

    len_ffn = 2 * len(FF_BLOCKS)
    _interleave((stage_mix(0), 2))
    for i in range(nsub):
        work = [(stage_ffn(i), len_ffn)]
        if i + 1 < nsub:
            work.append((stage_mix(i + 1), 2))
        if i > 0:
            work.append((stage_out(i - 1), 1))
        _interleave(*work)
    _interleave((stage_out(nsub - 1), 1))


def _out_ffn(layer, x, yc, yr, mod, wout, g1, b1, wfi, wfo, g2, b2):
    bsz, seq, d = x.shape
    ts = min(TS_OUT, seq)
    lay = lambda shape: pl.BlockSpec((None,) + shape, lambda b, t: (layer,) + (0,) * len(shape),
                                     pipeline_mode=pl.Buffered(1))
    modv = lambda i: pl.BlockSpec((None, None, None, 1, d), lambda b, t: (layer, b, i, 0, 0))
    return pl.pallas_call(
        _out_ffn_kernel,
        grid=(bsz, seq // ts),
        in_specs=[
            pl.BlockSpec((1, ts, d), lambda b, t: (b, t, 0)),
            pl.BlockSpec((1, ts, CONV_WIDTH), lambda b, t: (b, t, 0)),
            pl.BlockSpec((1, ts, RWKV_WIDTH), lambda b, t: (b, t, 0)),
            modv(MOD_GATE1), modv(MOD_SHIFT2), modv(MOD_SCALE2), modv(MOD_GATE2),
            lay((d, d)), lay((1, d)), lay((1, d)),
            lay((d, 2 * D_FF)), lay((D_FF, d)), lay((1, d)), lay((1, d)),
        ],
        out_specs=pl.BlockSpec((1, ts, d), lambda b, t: (b, t, 0)),
        out_shape=jax.ShapeDtypeStruct((bsz, seq, d), F32),
        compiler_params=pltpu.CompilerParams(dimension_semantics=("parallel", "parallel"),
                                             vmem_limit_bytes=VMEM_LIMIT),
        name="out_ffn",
    )(x, yc, yr, mod, mod, mod, mod, wout, g1, b1, wfi, wfo, g2, b2)


def kernel(x, c, w_mod, b_mod, w_in, conv_w, mu_shift, w0, w_decay_up, a0, a_up, g_up, k_k, k_a, r_k,
           lnx_g, lnx_b, w_out, ln1_g, ln1_b, w_ffn_in, w_ffn_out, ln2_g, ln2_b):
    depth = w_mod.shape[0]
    bsz, _, d = x.shape
    mod = _modulation(c, w_mod, b_mod).reshape(depth, bsz, N_MOD, 1, d)

    head = jnp.arange(RWKV_WIDTH) // HEAD_SIZE
    ones_bd = (head[:, None] == head[None, :]).astype(BF16)
    zeros_half = jnp.zeros((depth, LORA_WA // 2, RWKV_WIDTH), F32)
    wd = jnp.concatenate([w_decay_up, zeros_half], axis=1).astype(BF16)
    au = jnp.concatenate([zeros_half, a_up], axis=1).astype(BF16)
    vec = lambda p: p.reshape(depth, 1, -1)
    win, gu, wout, wfi, wfo = (p.astype(BF16) for p in (w_in, g_up, w_out, w_ffn_in, w_ffn_out))
    mu, w0_, a0_, kk_, ka_, rk_, lng, lnb = (vec(p) for p in (mu_shift, w0, a0, k_k, k_a, r_k, lnx_g, lnx_b))
    g1, b1, g2, b2 = (vec(p) for p in (ln1_g, ln1_b, ln2_g, ln2_b))

    for l in range(depth):
        yc, r, lw, k, v, kkn, bv, g = _inproj(l, x, mod, win, conv_w, mu, w0_, a0_, kk_, ka_, wd, au, gu, ones_bd)
        yr = _rwkv(l, r, lw, k, v, kkn, bv, g, rk_, lng, lnb, ones_bd)
        x = _out_ffn(l, x, yc, yr, mod, wout, g1, b1, wfi, wfo, g2, b2)
    return x
```

```python
import jax
import jax.numpy as jnp
from jax import lax
from jax.experimental import pallas as pl
from jax.experimental.pallas import tpu as pltpu

F32 = jnp.float32
BF16 = jnp.bfloat16

D_MODEL = 1024
DEPTH = 4
CONV_WIDTH = 512
RWKV_WIDTH = 512
HEAD_SIZE = 64
N_HEADS = RWKV_WIDTH // HEAD_SIZE
LORA_WA = 128
LORA_G = 128
D_FF = 2816
N_MOD = 6
MOD_SHIFT1, MOD_SCALE1, MOD_GATE1, MOD_SHIFT2, MOD_SCALE2, MOD_GATE2 = range(N_MOD)
N_CONV_COLS = 3 * CONV_WIDTH
N_SHIFT_COLS = 3 * RWKV_WIDTH + LORA_WA + LORA_G
N_IN_COLS = N_CONV_COLS + N_SHIFT_COLS
ALPHA = (2.0 * DEPTH) ** 0.25
LN_EPS = 1e-5
GN_EPS = 64e-5

CHUNK = 64
GROUP = 256
HEADS_PER_GROUP = GROUP // HEAD_SIZE
N_GROUPS = RWKV_WIDTH // GROUP
SUBLANES = 8

TS_IN = 512
TS_RWKV = 256
TS_OUT = 1024
SUB_ROWS = 256
MXU_WIDTH = 256
FF_BLOCKS = (6 * MXU_WIDTH, 5 * MXU_WIDTH)
MOD_TN = 1536
VMEM_LIMIT = 56 * 1024 * 1024


def _nn(a, b):
    return lax.dot_general(a, b, (((1,), (0,)), ((), ())), preferred_element_type=F32)


def _nt(a, b):
    return lax.dot_general(a, b, (((1,), (1,)), ((), ())), preferred_element_type=F32)


def _tn(a, b):
    return lax.dot_general(a, b, (((0,), (0,)), ((), ())), preferred_element_type=F32)


def _sigmoid(x):
    return 1.0 / (1.0 + jnp.exp(-x))


def _softplus(x):
    return jnp.maximum(x, 0.0) + jnp.log(1.0 + jnp.exp(-jnp.abs(x)))


def _layer_norm(x, g, b):
    mu = jnp.mean(x, axis=-1, keepdims=True)
    d = x - mu
    var = jnp.mean(d * d, axis=-1, keepdims=True)
    return d * lax.rsqrt(var + LN_EPS) * g + b


def _interleave(*stages):
    done = [0] * len(stages)
    alive = [True] * len(stages)
    while any(alive):
        i = min((j for j in range(len(stages)) if alive[j]), key=lambda j: done[j] / stages[j][1])
        try:
            next(stages[i][0])
            done[i] += 1
        except StopIteration:
            alive[i] = False


def _mod_kernel(c_ref, w_ref, b_ref, o_ref):
    c = c_ref[...]
    s = c * _sigmoid(c)
    o_ref[0] = lax.dot_general(s, w_ref[0], (((1,), (0,)), ((), ())), precision=lax.Precision.HIGHEST,
                               preferred_element_type=F32) + b_ref[0]


def _modulation(c, w_mod, b_mod):
    bsz, d = c.shape
    depth, _, n = w_mod.shape
    rows = -(-bsz // SUBLANES) * SUBLANES
    c_pad = jnp.zeros((rows, d), F32).at[:bsz].set(c)
    out = pl.pallas_call(
        _mod_kernel,
        grid=(depth, n // MOD_TN),
        in_specs=[
            pl.BlockSpec((rows, d), lambda l, j: (0, 0)),
            pl.BlockSpec((1, d, MOD_TN), lambda l, j: (l, 0, j)),
            pl.BlockSpec((1, 1, MOD_TN), lambda l, j: (l, 0, j)),
        ],
        out_specs=pl.BlockSpec((1, rows, MOD_TN), lambda l, j: (l, 0, j)),
        out_shape=jax.ShapeDtypeStruct((depth, rows, n), F32),
        compiler_params=pltpu.CompilerParams(dimension_semantics=("arbitrary", "arbitrary"),
                                             vmem_limit_bytes=VMEM_LIMIT),
        name="modulation",
    )(c_pad, w_mod, b_mod.reshape(depth, 1, n))
    return out[:, :bsz]


def _inproj_kernel(x_ref, sh_ref, sc_ref, win_ref, convw_ref, mu_ref, w0_ref, a0_ref, kk_ref, ka_ref,
                   wd_ref, au_ref, gu_ref, ones_ref,
                   yconv_ref, r_ref, lw_ref, k_ref, v_ref, kkn_ref, bv_ref, g_ref,
                   pbuf, zbuf):
    ts = x_ref.shape[1]

    @pl.when(pl.program_id(1) == 0)
    def _():
        pbuf[...] = jnp.zeros((SUBLANES, N_SHIFT_COLS), F32)
        zbuf[...] = jnp.zeros((SUBLANES, CONV_WIDTH), F32)

    sub = min(SUB_ROWS, ts)
    nsub = ts // sub
    ps_ = {}
    w3 = 3 * RWKV_WIDTH
    tails = {"p": pbuf[...], "z": zbuf[...]}

    def shifted(cur, tail, k):
        rolled = pltpu.roll(cur, k, axis=0)
        row8 = lax.broadcasted_iota(jnp.int32, tail.shape, 0)
        head = jnp.where(row8 < k, pltpu.roll(tail, k, axis=0), rolled[0:SUBLANES])
        return jnp.concatenate([head, rolled[SUBLANES:]], axis=0)

    def stage_project(i):
        rows = slice(i * sub, (i + 1) * sub)
        h = x_ref[0, rows, :] * (1.0 + sc_ref[...]) + sh_ref[...]
        ps_[i] = _nn(h.astype(BF16), win_ref[...])
        yield

    def stage_mixers(i):
        p = ps_[i]
        rows = slice(i * sub, (i + 1) * sub)
        bg = p[:, 0:CONV_WIDTH]
        z = p[:, CONV_WIDTH:2 * CONV_WIDTH] * p[:, 2 * CONV_WIDTH:3 * CONV_WIDTH]
        z1 = shifted(z, tails["z"], 1)
        z2 = shifted(z, tails["z"], 2)
        tails["z"] = z[sub - SUBLANES:sub]
        cw = convw_ref[...]
        yconv_ref[0, rows, :] = bg * (cw[0:1] * z2 + cw[1:2] * z1 + cw[2:3] * z)
        yield

        ps_raw = p[:, N_CONV_COLS:N_IN_COLS]
        prev = shifted(ps_raw, tails["p"], 1)
        tails["p"] = ps_raw[sub - SUBLANES:sub]
        ps = ps_raw + mu_ref[...] * (prev - ps_raw)
        r = ps[:, 0:RWKV_WIDTH]
        k = ps[:, RWKV_WIDTH:2 * RWKV_WIDTH]
        wa = ps[:, w3:w3 + LORA_WA]
        gl = ps[:, w3 + LORA_WA:w3 + LORA_WA + LORA_G]
        r_ref[0, rows, :] = r
        v_ref[0, rows, :] = ps[:, 2 * RWKV_WIDTH:w3]
        yield

        dw = _nn(jnp.tanh(wa).astype(BF16), wd_ref[...])
        da = _nn(wa.astype(BF16), au_ref[...])
        w_log = -_softplus(-(w0_ref[...] + dw)) - 0.5
        lw_ref[0, rows, :] = -jnp.exp(w_log)
        a = _sigmoid(a0_ref[...] + da)
        g_ref[0, rows, :] = _nn(_sigmoid(gl).astype(BF16), gu_ref[...])
        yield

        kkv = k * kk_ref[...]
        ss = _nn((kkv * kkv).astype(BF16), ones_ref[...])
        kkn = kkv * lax.rsqrt(jnp.maximum(ss, 1e-24))
        k_ref[0, rows, :] = k * (1.0 + (a - 1.0) * ka_ref[...])
        kkn_ref[0, rows, :] = kkn
        bv_ref[0, rows, :] = kkn * a
        yield

    _interleave((stage_project(0), 1))
    for i in range(nsub):
        work = [(stage_mixers(i), 4)]
        if i + 1 < nsub:
            work.append((stage_project(i + 1), 1))
        _interleave(*work)
    zbuf[...] = tails["z"]
    pbuf[...] = tails["p"]


def _inproj(layer, x, mod, win, convw, mu, w0, a0, k_k, k_a, wd, au, gu, ones_bd):
    bsz, seq, d = x.shape
    ts = min(TS_IN, seq)
    const = lambda shape: pl.BlockSpec(shape, lambda b, t: (0,) * len(shape), pipeline_mode=pl.Buffered(1))
    lay = lambda shape: pl.BlockSpec((None,) + shape, lambda b, t: (layer,) + (0,) * len(shape),
                                     pipeline_mode=pl.Buffered(1))
    modv = lambda i: pl.BlockSpec((None, None, None, 1, d), lambda b, t: (layer, b, i, 0, 0))
    row = pl.BlockSpec((1, ts, RWKV_WIDTH), lambda b, t: (b, t, 0))
    out_sds = jax.ShapeDtypeStruct((bsz, seq, RWKV_WIDTH), F32)
    return pl.pallas_call(
        _inproj_kernel,
        grid=(bsz, seq // ts),
        in_specs=[
            pl.BlockSpec((1, ts, d), lambda b, t: (b, t, 0)),
            modv(MOD_SHIFT1), modv(MOD_SCALE1),
            lay((d, N_IN_COLS)),
            lay((3, CONV_WIDTH)),
            lay((1, N_SHIFT_COLS)),
            lay((1, RWKV_WIDTH)), lay((1, RWKV_WIDTH)), lay((1, RWKV_WIDTH)), lay((1, RWKV_WIDTH)),
            lay((LORA_WA, RWKV_WIDTH)), lay((LORA_WA, RWKV_WIDTH)), lay((LORA_G, RWKV_WIDTH)),
            const((RWKV_WIDTH, RWKV_WIDTH)),
        ],
        out_specs=[row] * 8,
        out_shape=[out_sds] * 8,
        scratch_shapes=[pltpu.VMEM((SUBLANES, N_SHIFT_COLS), F32),
                        pltpu.VMEM((SUBLANES, CONV_WIDTH), F32)],
        compiler_params=pltpu.CompilerParams(dimension_semantics=("arbitrary", "arbitrary"),
                                             vmem_limit_bytes=VMEM_LIMIT),
        name="inproj",
    )(x, mod, mod, win, convw, mu, w0, a0, k_k, k_a, wd, au, gu, ones_bd)


def _rwkv_kernel(r_ref, lw_ref, k_ref, v_ref, kkn_ref, bv_ref, g_ref, rk_ref, lng_ref, lnb_ref, ones_ref,
                 y_ref, s_ref):
    nb, ts, _ = r_ref.shape
    c = CHUNK
    nck = ts // c

    @pl.when(pl.program_id(0) == 0)
    def _():
        s_ref[...] = jnp.zeros(s_ref.shape, F32)

    rr = lax.broadcasted_iota(jnp.int32, (GROUP, GROUP), 0) // HEAD_SIZE
    cc = lax.broadcasted_iota(jnp.int32, (GROUP, GROUP), 1) // HEAD_SIZE
    bd = rr == cc
    bd_bf = bd.astype(BF16)
    ti = lax.broadcasted_iota(jnp.int32, (c, GROUP), 0)
    tj = lax.broadcasted_iota(jnp.int32, (c, GROUP), 1) % c
    strict = ti > tj
    incl = ti >= tj
    eye = (ti == tj).astype(F32)

    def level_mask(s):
        return (ti // (2 * s) == tj // (2 * s)) & (ti % (2 * s) >= s) & (tj % (2 * s) < s)

    trow = lax.broadcasted_iota(jnp.int32, (c, RWKV_WIDTH), 0)

    def vt(x):
        xb = x.astype(BF16)
        return jnp.concatenate([xb] * HEADS_PER_GROUP, axis=0) * bd_bf

    lanes = lambda gi: slice(gi * GROUP, (gi + 1) * GROUP)

    prep, st, ys = {}, {}, {}
    seqs = [(b, gi) for b in range(nb) for gi in range(N_GROUPS)]
    state = {sq: s_ref[sq[0], sq[1]] for sq in seqs}


    def stage_operands(ci):
        for b in range(nb):
            rows = slice(ci * c, (ci + 1) * c)
            lw = lw_ref[b, rows, :]
            k = k_ref[b, rows, :]
            kkn = kkn_ref[b, rows, :]
            bv = bv_ref[b, rows, :]
            cl = lw
            dist = 1
            while dist < c:
                cl = cl + jnp.where(trow >= dist, pltpu.roll(cl, dist, axis=0), 0.0)
                dist *= 2
            cl_end = cl[c - 1:c, :]
            e_neg = jnp.exp(-cl)
            e_end = jnp.exp(cl_end - cl)
            p = dict(
                rt=(r_ref[b, rows, :] * jnp.exp(cl)).astype(BF16),
                at=(-kkn * jnp.exp(cl - lw)).astype(BF16),
                kt=(k * e_neg).astype(BF16),
                bt=(bv * e_neg).astype(BF16),
                kh=(k * e_end).astype(BF16),
                bh=(bv * e_end).astype(BF16),
                v=v_ref[b, rows, :].astype(BF16),
                w_end=jnp.exp(cl_end))
            prep[(b, ci)] = p
            yield
            for gi in range(N_GROUPS):
                ln = lanes(gi)
                ar = jnp.concatenate([p["at"][:, ln], p["rt"][:, ln]], axis=0)
                pb = _nt(ar, vt(p["bt"][:, ln]))
                pk = _nt(ar, vt(p["kt"][:, ln]))
                a_ab = jnp.where(strict, pb[0:c], 0.0)
                st[(b, ci, gi)] = dict(
                    a_ab=a_ab,
                    a_ak=jnp.where(strict, pk[0:c], 0.0).astype(BF16),
                    m_rb=jnp.where(incl, pb[c:2 * c], 0.0).astype(BF16),
                    m_rk=jnp.where(incl, pk[c:2 * c], 0.0).astype(BF16),
                    vt_v=vt(p["v"][:, ln]),
                    t_inv=eye + jnp.where(level_mask(1), a_ab, 0.0))
                yield

    def stage_solve(ci):
        chs = [(b, ci, gi) for (b, gi) in seqs]
        s = 2
        while s < c:
            lm = level_mask(s)
            xs = {}
            for ch in chs:
                xs[ch] = _nn(jnp.where(lm, st[ch]["a_ab"], 0.0).astype(BF16), vt(st[ch]["t_inv"]))
                yield
            for ch in chs:
                st[ch]["t_inv"] = st[ch]["t_inv"] + _nn(st[ch]["t_inv"].astype(BF16), vt(xs[ch]))
                yield
            s *= 2
        for ch in chs:
            d = st[ch]
            avy = _nn(jnp.concatenate([d["a_ak"], d["m_rk"]], axis=0), d["vt_v"])
            d["av"] = avy[0:c]
            d["yloc"] = avy[c:2 * c]
            d["tb"] = d["t_inv"].astype(BF16)
            yield

    def stage_state(ci):
        res, sa, upd = {}, {}, {}
        for (b, gi) in seqs:
            p = prep[(b, ci)]
            lhs = jnp.concatenate([p["at"][:, lanes(gi)], p["rt"][:, lanes(gi)]], axis=0)
            res[(b, gi)] = _nt(lhs, state[(b, gi)].astype(BF16))
            yield
        for (b, gi) in seqs:
            d = st[(b, ci, gi)]
            sa[(b, gi)] = _nn(d["tb"], vt(res[(b, gi)][0:c] + d["av"]))
            yield
        for (b, gi) in seqs:
            p = prep[(b, ci)]
            ln = lanes(gi)
            lhs = jnp.concatenate([sa[(b, gi)].astype(BF16), p["v"][:, ln]], axis=0)
            rhs = jnp.concatenate([p["bh"][:, ln], p["kh"][:, ln]], axis=0)
            upd[(b, gi)] = _tn(lhs, rhs)
            yield
        for (b, gi) in seqs:
            d = st[(b, ci, gi)]
            ys[(b, ci, gi)] = res[(b, gi)][c:2 * c] + _nn(d["m_rb"], vt(sa[(b, gi)])) + d["yloc"]
            state[(b, gi)] = (state[(b, gi)] * prep[(b, ci)]["w_end"][:, lanes(gi)]
                              + jnp.where(bd, upd[(b, gi)], 0.0))
            yield

    n_seq = len(seqs)
    n_levels = 0
    s = 2
    while s < c:
        n_levels += 1
        s *= 2
    len_operands = nb * (1 + N_GROUPS)
    len_solve = n_seq * (2 * n_levels + 1)
    len_state = 4 * n_seq

    ones = ones_ref[...]
    inv_n = 1.0 / HEAD_SIZE

    def stage_output(ci):
        rows = slice(ci * c, (ci + 1) * c)
        y = jnp.concatenate([jnp.concatenate([ys[(b, ci, gi)] for gi in range(N_GROUPS)], axis=1)
                             for b in range(nb)], axis=0)
        allrows = lambda ref: jnp.concatenate([ref[b, rows, :] for b in range(nb)], axis=0)
        mu = _nn(y.astype(BF16), ones) * inv_n
        yield
        dlt = y - mu
        var = _nn((dlt * dlt).astype(BF16), ones) * inv_n
        yield
        yn = dlt * lax.rsqrt(var + GN_EPS) * lng_ref[...] + lnb_ref[...]
        bonus = _nn((allrows(r_ref) * allrows(k_ref) * rk_ref[...]).astype(BF16), ones) * allrows(v_ref)
        yield
        out = (yn + bonus) * allrows(g_ref)
        for b in range(nb):
            y_ref[b, rows, :] = out[b * c:(b + 1) * c]
        yield

    len_output = 4

    _interleave((stage_operands(0), len_operands))
    for ci in range(nck):
        work = [(stage_solve(ci), len_solve)]
        if ci + 1 < nck:
            work.append((stage_operands(ci + 1), len_operands))
        if ci > 0:
            work.append((stage_state(ci - 1), len_state))
        if ci > 1:
            work.append((stage_output(ci - 2), len_output))
        _interleave(*work)
    work = [(stage_state(nck - 1), len_state)]
    if nck > 1:
        work.append((stage_output(nck - 2), len_output))
    _interleave(*work)
    _interleave((stage_output(nck - 1), len_output))
    for (b, gi) in seqs:
        s_ref[b, gi] = state[(b, gi)]


def _rwkv(layer, r, lw, k, v, kkn, bv, g, r_k, lnx_g, lnx_b, ones_bd):
    bsz, seq, w = r.shape
    ts = min(TS_RWKV, seq)
    row = pl.BlockSpec((bsz, ts, w), lambda t: (0, t, 0))
    const = lambda shape: pl.BlockSpec(shape, lambda t: (0,) * len(shape), pipeline_mode=pl.Buffered(1))
    lay = lambda shape: pl.BlockSpec((None,) + shape, lambda t: (layer,) + (0,) * len(shape),
                                     pipeline_mode=pl.Buffered(1))
    return pl.pallas_call(
        _rwkv_kernel,
        grid=(seq // ts,),
        in_specs=[row] * 7 + [lay((1, w))] * 3 + [const((w, w))],
        out_specs=row,
        out_shape=jax.ShapeDtypeStruct((bsz, seq, w), F32),
        scratch_shapes=[pltpu.VMEM((bsz, N_GROUPS, GROUP, GROUP), F32)],
        compiler_params=pltpu.CompilerParams(dimension_semantics=("arbitrary",),
                                             vmem_limit_bytes=VMEM_LIMIT),
        name="rwkv7_scan",
    )(r, lw, k, v, kkn, bv, g, r_k, lnx_g, lnx_b, ones_bd)


def _out_ffn_kernel(x_ref, yc_ref, yr_ref, gt1_ref, sh2_ref, sc2_ref, gt2_ref, wout_ref, g1_ref, b1_ref,
                    wfi_ref, wfo_ref, g2_ref, b2_ref, o_ref):
    ts = x_ref.shape[1]
    sub = min(SUB_ROWS, ts)
    nsub = ts // sub
    x1s, hbs, fs = {}, {}, {}

    def stage_mix(i):
        rows = slice(i * sub, (i + 1) * sub)
        mix = (_nn(yc_ref[0, rows, :].astype(BF16), wout_ref[0:CONV_WIDTH, :])
               + _nn(yr_ref[0, rows, :].astype(BF16), wout_ref[CONV_WIDTH:CONV_WIDTH + RWKV_WIDTH, :]))
        yield
        x1 = _layer_norm(ALPHA * x_ref[0, rows, :] + (1.0 + gt1_ref[...]) * mix, g1_ref[...], b1_ref[...])
        x1s[i] = x1
        hbs[i] = (x1 * (1.0 + sc2_ref[...]) + sh2_ref[...]).astype(BF16)
        yield

    def stage_ffn(i):
        acc = None
        j = 0
        for blk in FF_BLOCKS:
            gate = _nn(hbs[i], wfi_ref[:, j:j + blk])
            up = _nn(hbs[i], wfi_ref[:, D_FF + j:D_FF + j + blk])
            yield
            part = _nn((gate * _sigmoid(gate) * up).astype(BF16), wfo_ref[j:j + blk, :])
            acc = part if acc is None else acc + part
            j += blk
            yield
        fs[i] = acc

    def stage_out(i):
        rows = slice(i * sub, (i + 1) * sub)
        o_ref[0, rows, :] = _layer_norm(ALPHA * x1s[i] + (1.0 + gt2_ref[...]) * fs[i], g2_ref[...], b2_ref[...])
        yield

    len_ffn = 2 * len(FF_BLOCKS)
    _interleave((stage_mix(0), 2))
    for i in range(nsub):
        work = [(stage_ffn(i), len_ffn)]
        if i + 1 < nsub:
            work.append((stage_mix(i + 1), 2))
        if i > 0:
            work.append((stage_out(i - 1), 1))
        _interleave(*work)
    _interleave((stage_out(nsub - 1), 1))


def _out_ffn(layer, x, yc, yr, mod, wout, g1, b1, wfi, wfo, g2, b2):
    bsz, seq, d = x.shape
    ts = min(TS_OUT, seq)
    lay = lambda shape: pl.BlockSpec((None,) + shape, lambda b, t: (layer,) + (0,) * len(shape),
                                     pipeline_mode=pl.Buffered(1))
    modv = lambda i: pl.BlockSpec((None, None, None, 1, d), lambda b, t: (layer, b, i, 0, 0))
    return pl.pallas_call(
        _out_ffn_kernel,
        grid=(bsz, seq // ts),
        in_specs=[
            pl.BlockSpec((1, ts, d), lambda b, t: (b, t, 0)),
            pl.BlockSpec((1, ts, CONV_WIDTH), lambda b, t: (b, t, 0)),
            pl.BlockSpec((1, ts, RWKV_WIDTH), lambda b, t: (b, t, 0)),
            modv(MOD_GATE1), modv(MOD_SHIFT2), modv(MOD_SCALE2), modv(MOD_GATE2),
            lay((d, d)), lay((1, d)), lay((1, d)),
            lay((d, 2 * D_FF)), lay((D_FF, d)), lay((1, d)), lay((1, d)),
        ],
        out_specs=pl.BlockSpec((1, ts, d), lambda b, t: (b, t, 0)),
        out_shape=jax.ShapeDtypeStruct((bsz, seq, d), F32),
        compiler_params=pltpu.CompilerParams(dimension_semantics=("parallel", "parallel"),
                                             vmem_limit_bytes=VMEM_LIMIT),
        name="out_ffn",
    )(x, yc, yr, mod, mod, mod, mod, wout, g1, b1, wfi, wfo, g2, b2)


def kernel(x, c, w_mod, b_mod, w_in, conv_w, mu_shift, w0, w_decay_up, a0, a_up, g_up, k_k, k_a, r_k,
           lnx_g, lnx_b, w_out, ln1_g, ln1_b, w_ffn_in, w_ffn_out, ln2_g, ln2_b):
    depth = w_mod.shape[0]
    bsz, _, d = x.shape
    mod = _modulation(c, w_mod, b_mod).reshape(depth, bsz, N_MOD, 1, d)

    head = jnp.arange(RWKV_WIDTH) // HEAD_SIZE
    ones_bd = (head[:, None] == head[None, :]).astype(BF16)
    zeros_half = jnp.zeros((depth, LORA_WA // 2, RWKV_WIDTH), F32)
    wd = jnp.concatenate([w_decay_up, zeros_half], axis=1).astype(BF16)
    au = jnp.concatenate([zeros_half, a_up], axis=1).astype(BF16)
    vec = lambda p: p.reshape(depth, 1, -1)
    win, gu, wout, wfi, wfo = (p.astype(BF16) for p in (w_in, g_up, w_out, w_ffn_in, w_ffn_out))
    mu, w0_, a0_, kk_, ka_, rk_, lng, lnb = (vec(p) for p in (mu_shift, w0, a0, k_k, k_a, r_k, lnx_g, lnx_b))
    g1, b1, g2, b2 = (vec(p) for p in (ln1_g, ln1_b, ln2_g, ln2_b))

    for l in range(depth):
        yc, r, lw, k, v, kkn, bv, g = _inproj(l, x, mod, win, conv_w, mu, w0_, a0_, kk_, ka_, wd, au, gu, ones_bd)
        yr = _rwkv(l, r, lw, k, v, kkn, bv, g, rk_, lng, lnb, ones_bd)
        x = _out_ffn(l, x, yc, yr, mod, wout, g1, b1, wfi, wfo, g2, b2)
    return x
```

```python
import jax
import jax.numpy as jnp
from jax import lax
from jax.experimental import pallas as pl
from jax.experimental.pallas import tpu as pltpu

F32 = jnp.float32
BF16 = jnp.bfloat16

D_MODEL = 1024
DEPTH = 4
CONV_WIDTH = 512
RWKV_WIDTH = 512
HEAD_SIZE = 64
N_HEADS = RWKV_WIDTH // HEAD_SIZE
LORA_WA = 128
LORA_G = 128
D_FF = 2816
N_MOD = 6
MOD_SHIFT1, MOD_SCALE1, MOD_GATE1, MOD_SHIFT2, MOD_SCALE2, MOD_GATE2 = range(N_MOD)
N_CONV_COLS = 3 * CONV_WIDTH
N_SHIFT_COLS = 3 * RWKV_WIDTH + LORA_WA + LORA_G
N_IN_COLS = N_CONV_COLS + N_SHIFT_COLS
ALPHA = (2.0 * DEPTH) ** 0.25
LN_EPS = 1e-5
GN_EPS = 64e-5

CHUNK = 64
GROUP = 256
HEADS_PER_GROUP = GROUP // HEAD_SIZE
N_GROUPS = RWKV_WIDTH // GROUP
SUBLANES = 8

TS_IN = 1024
TS_RWKV = 256
TS_OUT = 1024
SUB_ROWS = 256
MXU_WIDTH = 256
FF_BLOCKS = (6 * MXU_WIDTH, 5 * MXU_WIDTH)
MOD_TN = 1536
VMEM_LIMIT = 56 * 1024 * 1024


def _nn(a, b):
    return lax.dot_general(a, b, (((1,), (0,)), ((), ())), preferred_element_type=F32)


def _nt(a, b):
    return lax.dot_general(a, b, (((1,), (1,)), ((), ())), preferred_element_type=F32)


def _tn(a, b):
    return lax.dot_general(a, b, (((0,), (0,)), ((), ())), preferred_element_type=F32)


def _sigmoid(x):
    return 1.0 / (1.0 + jnp.exp(-x))


def _softplus(x):
    return jnp.maximum(x, 0.0) + jnp.log(1.0 + jnp.exp(-jnp.abs(x)))


def _layer_norm(x, g, b):
    mu = jnp.mean(x, axis=-1, keepdims=True)
    d = x - mu
    var = jnp.mean(d * d, axis=-1, keepdims=True)
    return d * lax.rsqrt(var + LN_EPS) * g + b


def _interleave(*stages):
    done = [0] * len(stages)
    alive = [True] * len(stages)
    while any(alive):
        i = min((j for j in range(len(stages)) if alive[j]), key=lambda j: done[j] / stages[j][1])
        try:
            next(stages[i][0])
            done[i] += 1
        except StopIteration:
            alive[i] = False


def _mod_kernel(c_ref, w_ref, b_ref, o_ref):
    c = c_ref[...]
    s = c * _sigmoid(c)
    o_ref[0] = lax.dot_general(s, w_ref[0], (((1,), (0,)), ((), ())), precision=lax.Precision.HIGHEST,
                               preferred_element_type=F32) + b_ref[0]


def _modulation(c, w_mod, b_mod):
    bsz, d = c.shape
    depth, _, n = w_mod.shape
    rows = -(-bsz // SUBLANES) * SUBLANES
    c_pad = jnp.zeros((rows, d), F32).at[:bsz].set(c)
    out = pl.pallas_call(
        _mod_kernel,
        grid=(depth, n // MOD_TN),
        in_specs=[
            pl.BlockSpec((rows, d), lambda l, j: (0, 0)),
            pl.BlockSpec((1, d, MOD_TN), lambda l, j: (l, 0, j)),
            pl.BlockSpec((1, 1, MOD_TN), lambda l, j: (l, 0, j)),
        ],
        out_specs=pl.BlockSpec((1, rows, MOD_TN), lambda l, j: (l, 0, j)),
        out_shape=jax.ShapeDtypeStruct((depth, rows, n), F32),
        compiler_params=pltpu.CompilerParams(dimension_semantics=("arbitrary", "arbitrary"),
                                             vmem_limit_bytes=VMEM_LIMIT),
        name="modulation",
    )(c_pad, w_mod, b_mod.reshape(depth, 1, n))
    return out[:, :bsz]


def _inproj_kernel(x_ref, sh_ref, sc_ref, win_ref, convw_ref, mu_ref, w0_ref, a0_ref, kk_ref, ka_ref,
                   wd_ref, au_ref, gu_ref, ones_ref,
                   yconv_ref, r_ref, lw_ref, k_ref, v_ref, kkn_ref, bv_ref, g_ref,
                   pbuf, zbuf):
    ts = x_ref.shape[1]

    @pl.when(pl.program_id(1) == 0)
    def _():
        pbuf[...] = jnp.zeros((SUBLANES, N_SHIFT_COLS), F32)
        zbuf[...] = jnp.zeros((SUBLANES, CONV_WIDTH), F32)

    sub = min(SUB_ROWS, ts)
    nsub = ts // sub
    ps_ = {}
    w3 = 3 * RWKV_WIDTH
    tails = {"p": pbuf[...], "z": zbuf[...]}

    def shifted(cur, tail, k):
        rolled = pltpu.roll(cur, k, axis=0)
        row8 = lax.broadcasted_iota(jnp.int32, tail.shape, 0)
        head = jnp.where(row8 < k, pltpu.roll(tail, k, axis=0), rolled[0:SUBLANES])
        return jnp.concatenate([head, rolled[SUBLANES:]], axis=0)

    def stage_project(i):
        rows = slice(i * sub, (i + 1) * sub)
        h = x_ref[0, rows, :] * (1.0 + sc_ref[...]) + sh_ref[...]
        ps_[i] = _nn(h.astype(BF16), win_ref[...])
        yield

    def stage_mixers(i):
        p = ps_[i]
        rows = slice(i * sub, (i + 1) * sub)
        bg = p[:, 0:CONV_WIDTH]
        z = p[:, CONV_WIDTH:2 * CONV_WIDTH] * p[:, 2 * CONV_WIDTH:3 * CONV_WIDTH]
        z1 = shifted(z, tails["z"], 1)
        z2 = shifted(z, tails["z"], 2)
        tails["z"] = z[sub - SUBLANES:sub]
        cw = convw_ref[...]
        yconv_ref[0, rows, :] = bg * (cw[0:1] * z2 + cw[1:2] * z1 + cw[2:3] * z)
        yield

        ps_raw = p[:, N_CONV_COLS:N_IN_COLS]
        prev = shifted(ps_raw, tails["p"], 1)
        tails["p"] = ps_raw[sub - SUBLANES:sub]
        ps = ps_raw + mu_ref[...] * (prev - ps_raw)
        r = ps[:, 0:RWKV_WIDTH]
        k = ps[:, RWKV_WIDTH:2 * RWKV_WIDTH]
        wa = ps[:, w3:w3 + LORA_WA]
        gl = ps[:, w3 + LORA_WA:w3 + LORA_WA + LORA_G]
        r_ref[0, rows, :] = r
        v_ref[0, rows, :] = ps[:, 2 * RWKV_WIDTH:w3]
        yield

        dw = _nn(jnp.tanh(wa).astype(BF16), wd_ref[...])
        da = _nn(wa.astype(BF16), au_ref[...])
        w_log = -_softplus(-(w0_ref[...] + dw)) - 0.5
        lw_ref[0, rows, :] = -jnp.exp(w_log)
        a = _sigmoid(a0_ref[...] + da)
        g_ref[0, rows, :] = _nn(_sigmoid(gl).astype(BF16), gu_ref[...])
        yield

        kkv = k * kk_ref[...]
        ss = _nn((kkv * kkv).astype(BF16), ones_ref[...])
        kkn = kkv * lax.rsqrt(jnp.maximum(ss, 1e-24))
        k_ref[0, rows, :] = k * (1.0 + (a - 1.0) * ka_ref[...])
        kkn_ref[0, rows, :] = kkn
        bv_ref[0, rows, :] = kkn * a
        yield

    _interleave((stage_project(0), 1))
    for i in range(nsub):
        work = [(stage_mixers(i), 4)]
        if i + 1 < nsub:
            work.append((stage_project(i + 1), 1))
        _interleave(*work)
    zbuf[...] = tails["z"]
    pbuf[...] = tails["p"]


def _inproj(layer, x, mod, win, convw, mu, w0, a0, k_k, k_a, wd, au, gu, ones_bd):
    bsz, seq, d = x.shape
    ts = min(TS_IN, seq)
    const = lambda shape: pl.BlockSpec(shape, lambda b, t: (0,) * len(shape), pipeline_mode=pl.Buffered(1))
    lay = lambda shape: pl.BlockSpec((None,) + shape, lambda b, t: (layer,) + (0,) * len(shape),
                                     pipeline_mode=pl.Buffered(1))
    modv = lambda i: pl.BlockSpec((None, None, None, 1, d), lambda b, t: (layer, b, i, 0, 0))
    row = pl.BlockSpec((1, ts, RWKV_WIDTH), lambda b, t: (b, t, 0))
    out_sds = jax.ShapeDtypeStruct((bsz, seq, RWKV_WIDTH), F32)
    return pl.pallas_call(
        _inproj_kernel,
        grid=(bsz, seq // ts),
        in_specs=[
            pl.BlockSpec((1, ts, d), lambda b, t: (b, t, 0)),
            modv(MOD_SHIFT1), modv(MOD_SCALE1),
            lay((d, N_IN_COLS)),
            lay((3, CONV_WIDTH)),
            lay((1, N_SHIFT_COLS)),
            lay((1, RWKV_WIDTH)), lay((1, RWKV_WIDTH)), lay((1, RWKV_WIDTH)), lay((1, RWKV_WIDTH)),
            lay((LORA_WA, RWKV_WIDTH)), lay((LORA_WA, RWKV_WIDTH)), lay((LORA_G, RWKV_WIDTH)),
            const((RWKV_WIDTH, RWKV_WIDTH)),
        ],
        out_specs=[row] * 8,
        out_shape=[out_sds] * 8,
        scratch_shapes=[pltpu.VMEM((SUBLANES, N_SHIFT_COLS), F32),
                        pltpu.VMEM((SUBLANES, CONV_WIDTH), F32)],
        compiler_params=pltpu.CompilerParams(dimension_semantics=("arbitrary", "arbitrary"),
                                             vmem_limit_bytes=VMEM_LIMIT),
        name="inproj",
    )(x, mod, mod, win, convw, mu, w0, a0, k_k, k_a, wd, au, gu, ones_bd)


def _rwkv_kernel(r_ref, lw_ref, k_ref, v_ref, kkn_ref, bv_ref, g_ref, rk_ref, lng_ref, lnb_ref, ones_ref,
                 y_ref, s_ref):
    nb, ts, _ = r_ref.shape
    c = CHUNK
    nck = ts // c

    @pl.when(pl.program_id(0) == 0)
    def _():
        s_ref[...] = jnp.zeros(s_ref.shape, F32)

    rr = lax.broadcasted_iota(jnp.int32, (GROUP, GROUP), 0) // HEAD_SIZE
    cc = lax.broadcasted_iota(jnp.int32, (GROUP, GROUP), 1) // HEAD_SIZE
    bd = rr == cc
    bd_bf = bd.astype(BF16)
    ti = lax.broadcasted_iota(jnp.int32, (c, GROUP), 0)
    tj = lax.broadcasted_iota(jnp.int32, (c, GROUP), 1) % c
    strict = ti > tj
    incl = ti >= tj
    eye = (ti == tj).astype(F32)

    def level_mask(s):
        return (ti // (2 * s) == tj // (2 * s)) & (ti % (2 * s) >= s) & (tj % (2 * s) < s)

    trow = lax.broadcasted_iota(jnp.int32, (c, RWKV_WIDTH), 0)

    def vt(x):
        xb = x.astype(BF16)
        return jnp.concatenate([xb] * HEADS_PER_GROUP, axis=0) * bd_bf

    lanes = lambda gi: slice(gi * GROUP, (gi + 1) * GROUP)

    prep, st, ys = {}, {}, {}
    seqs = [(b, gi) for b in range(nb) for gi in range(N_GROUPS)]
    state = {sq: s_ref[sq[0], sq[1]] for sq in seqs}


    def stage_operands(ci):
        for b in range(nb):
            rows = slice(ci * c, (ci + 1) * c)
            lw = lw_ref[b, rows, :]
            k = k_ref[b, rows, :]
            kkn = kkn_ref[b, rows, :]
            bv = bv_ref[b, rows, :]
            cl = lw
            dist = 1
            while dist < c:
                cl = cl + jnp.where(trow >= dist, pltpu.roll(cl, dist, axis=0), 0.0)
                dist *= 2
            cl_end = cl[c - 1:c, :]
            e_neg = jnp.exp(-cl)
            e_end = jnp.exp(cl_end - cl)
            p = dict(
                rt=(r_ref[b, rows, :] * jnp.exp(cl)).astype(BF16),
                at=(-kkn * jnp.exp(cl - lw)).astype(BF16),
                kt=(k * e_neg).astype(BF16),
                bt=(bv * e_neg).astype(BF16),
                kh=(k * e_end).astype(BF16),
                bh=(bv * e_end).astype(BF16),
                v=v_ref[b, rows, :].astype(BF16),
                w_end=jnp.exp(cl_end))
            prep[(b, ci)] = p
            yield
            for gi in range(N_GROUPS):
                ln = lanes(gi)
                ar = jnp.concatenate([p["at"][:, ln], p["rt"][:, ln]], axis=0)
                pb = _nt(ar, vt(p["bt"][:, ln]))
                pk = _nt(ar, vt(p["kt"][:, ln]))
                a_ab = jnp.where(strict, pb[0:c], 0.0)
                st[(b, ci, gi)] = dict(
                    a_ab=a_ab,
                    a_ak=jnp.where(strict, pk[0:c], 0.0).astype(BF16),
                    m_rb=jnp.where(incl, pb[c:2 * c], 0.0).astype(BF16),
                    m_rk=jnp.where(incl, pk[c:2 * c], 0.0).astype(BF16),
                    vt_v=vt(p["v"][:, ln]),
                    t_inv=eye + jnp.where(level_mask(1), a_ab, 0.0))
                yield

    def stage_solve(ci):
        chs = [(b, ci, gi) for (b, gi) in seqs]
        s = 2
        while s < c:
            lm = level_mask(s)
            xs = {}
            for ch in chs:
                xs[ch] = _nn(jnp.where(lm, st[ch]["a_ab"], 0.0).astype(BF16), vt(st[ch]["t_inv"]))
                yield
            for ch in chs:
                st[ch]["t_inv"] = st[ch]["t_inv"] + _nn(st[ch]["t_inv"].astype(BF16), vt(xs[ch]))
                yield
            s *= 2
        for ch in chs:
            d = st[ch]
            avy = _nn(jnp.concatenate([d["a_ak"], d["m_rk"]], axis=0), d["vt_v"])
            d["av"] = avy[0:c]
            d["yloc"] = avy[c:2 * c]
            d["tb"] = d["t_inv"].astype(BF16)
            yield

    def stage_state(ci):
        res, sa, upd = {}, {}, {}
        for (b, gi) in seqs:
            p = prep[(b, ci)]
            lhs = jnp.concatenate([p["at"][:, lanes(gi)], p["rt"][:, lanes(gi)]], axis=0)
            res[(b, gi)] = _nt(lhs, state[(b, gi)].astype(BF16))
            yield
        for (b, gi) in seqs:
            d = st[(b, ci, gi)]
            sa[(b, gi)] = _nn(d["tb"], vt(res[(b, gi)][0:c] + d["av"]))
            yield
        for (b, gi) in seqs:
            p = prep[(b, ci)]
            ln = lanes(gi)
            lhs = jnp.concatenate([sa[(b, gi)].astype(BF16), p["v"][:, ln]], axis=0)
            rhs = jnp.concatenate([p["bh"][:, ln], p["kh"][:, ln]], axis=0)
            upd[(b, gi)] = _tn(lhs, rhs)
            yield
        for (b, gi) in seqs:
            d = st[(b, ci, gi)]
            ys[(b, ci, gi)] = res[(b, gi)][c:2 * c] + _nn(d["m_rb"], vt(sa[(b, gi)])) + d["yloc"]
            state[(b, gi)] = (state[(b, gi)] * prep[(b, ci)]["w_end"][:, lanes(gi)]
                              + jnp.where(bd, upd[(b, gi)], 0.0))
            yield

    n_seq = len(seqs)
    n_levels = 0
    s = 2
    while s < c:
        n_levels += 1
        s *= 2
    len_operands = nb * (1 + N_GROUPS)
    len_solve = n_seq * (2 * n_levels + 1)
    len_state = 4 * n_seq

    ones = ones_ref[...]
    inv_n = 1.0 / HEAD_SIZE

    def stage_output(ci):
        rows = slice(ci * c, (ci + 1) * c)
        y = jnp.concatenate([jnp.concatenate([ys[(b, ci, gi)] for gi in range(N_GROUPS)], axis=1)
                             for b in range(nb)], axis=0)
        allrows = lambda ref: jnp.concatenate([ref[b, rows, :] for b in range(nb)], axis=0)
        mu = _nn(y.astype(BF16), ones) * inv_n
        yield
        dlt = y - mu
        var = _nn((dlt * dlt).astype(BF16), ones) * inv_n
        yield
        yn = dlt * lax.rsqrt(var + GN_EPS) * lng_ref[...] + lnb_ref[...]
        bonus = _nn((allrows(r_ref) * allrows(k_ref) * rk_ref[...]).astype(BF16), ones) * allrows(v_ref)
        yield
        out = (yn + bonus) * allrows(g_ref)
        for b in range(nb):
            y_ref[b, rows, :] = out[b * c:(b + 1) * c]
        yield

    len_output = 4

    _interleave((stage_operands(0), len_operands))
    for ci in range(nck):
        work = [(stage_solve(ci), len_solve)]
        if ci + 1 < nck:
            work.append((stage_operands(ci + 1), len_operands))
        if ci > 0:
            work.append((stage_state(ci - 1), len_state))
        if ci > 1:
            work.append((stage_output(ci - 2), len_output))
        _interleave(*work)
    work = [(stage_state(nck - 1), len_state)]
    if nck > 1:
        work.append((stage_output(nck - 2), len_output))
    _interleave(*work)
    _interleave((stage_output(nck - 1), len_output))
    for (b, gi) in seqs:
        s_ref[b, gi] = state[(b, gi)]


def _rwkv(layer, r, lw, k, v, kkn, bv, g, r_k, lnx_g, lnx_b, ones_bd):
    bsz, seq, w = r.shape
    ts = min(TS_RWKV, seq)
    row = pl.BlockSpec((bsz, ts, w), lambda t: (0, t, 0))
    const = lambda shape: pl.BlockSpec(shape, lambda t: (0,) * len(shape), pipeline_mode=pl.Buffered(1))
    lay = lambda shape: pl.BlockSpec((None,) + shape, lambda t: (layer,) + (0,) * len(shape),
                                     pipeline_mode=pl.Buffered(1))
    return pl.pallas_call(
        _rwkv_kernel,
        grid=(seq // ts,),
        in_specs=[row] * 7 + [lay((1, w))] * 3 + [const((w, w))],
        out_specs=row,
        out_shape=jax.ShapeDtypeStruct((bsz, seq, w), F32),
        scratch_shapes=[pltpu.VMEM((bsz, N_GROUPS, GROUP, GROUP), F32)],
        compiler_params=pltpu.CompilerParams(dimension_semantics=("arbitrary",),
                                             vmem_limit_bytes=VMEM_LIMIT),
        name="rwkv7_scan",
    )(r, lw, k, v, kkn, bv, g, r_k, lnx_g, lnx_b, ones_bd)


def _out_ffn_kernel(x_ref, yc_ref, yr_ref, gt1_ref, sh2_ref, sc2_ref, gt2_ref, wout_ref, g1_ref, b1_ref,
                    wfi_ref, wfo_ref, g2_ref, b2_ref, o_ref):
    ts = x_ref.shape[1]
    sub = min(SUB_ROWS, ts)
    nsub = ts // sub
    x1s, hbs, fs = {}, {}, {}

    def stage_mix(i):
        rows = slice(i * sub, (i + 1) * sub)
        mix = (_nn(yc_ref[0, rows, :].astype(BF16), wout_ref[0:CONV_WIDTH, :])
               + _nn(yr_ref[0, rows, :].astype(BF16), wout_ref[CONV_WIDTH:CONV_WIDTH + RWKV_WIDTH, :]))
        yield
        x1 = _layer_norm(ALPHA * x_ref[0, rows, :] + (1.0 + gt1_ref[...]) * mix, g1_ref[...], b1_ref[...])
        x1s[i] = x1
        hbs[i] = (x1 * (1.0 + sc2_ref[...]) + sh2_ref[...]).astype(BF16)
        yield

    def stage_ffn(i):
        acc = None
        j = 0
        for blk in FF_BLOCKS:
            gate = _nn(hbs[i], wfi_ref[:, j:j + blk])
            up = _nn(hbs[i], wfi_ref[:, D_FF + j:D_FF + j + blk])
            yield
            part = _nn((gate * _sigmoid(gate) * up).astype(BF16), wfo_ref[j:j + blk, :])
            acc = part if acc is None else acc + part
            j += blk
            yield
        fs[i] = acc

    def stage_out(i):
        rows = slice(i * sub, (i + 1) * sub)
        o_ref[0, rows, :] = _layer_norm(ALPHA * x1s[i] + (1.0 + gt2_ref[...]) * fs[i], g2_ref[...], b2_ref[...])
        yield

    len_ffn = 2 * len(FF_BLOCKS)
    _interleave((stage_mix(0), 2))
    for i in range(nsub):
        work = [(stage_ffn(i), len_ffn)]
        if i + 1 < nsub:
            work.append((stage_mix(i + 1), 2))
        if i > 0:
            work.append((stage_out(i - 1), 1))
        _interleave(*work)
    _interleave((stage_out(nsub - 1), 1))


def _out_ffn(layer, x, yc, yr, mod, wout, g1, b1, wfi, wfo, g2, b2):
    bsz, seq, d = x.shape
    ts = min(TS_OUT, seq)
    lay = lambda shape: pl.BlockSpec((None,) + shape, lambda b, t: (layer,) + (0,) * len(shape),
                                     pipeline_mode=pl.Buffered(1))
    modv = lambda i: pl.BlockSpec((None, None, None, 1, d), lambda b, t: (layer, b, i, 0, 0))
    return pl.pallas_call(
        _out_ffn_kernel,
        grid=(bsz, seq // ts),
        in_specs=[
            pl.BlockSpec((1, ts, d), lambda b, t: (b, t, 0)),
            pl.BlockSpec((1, ts, CONV_WIDTH), lambda b, t: (b, t, 0)),
            pl.BlockSpec((1, ts, RWKV_WIDTH), lambda b, t: (b, t, 0)),
            modv(MOD_GATE1), modv(MOD_SHIFT2), modv(MOD_SCALE2), modv(MOD_GATE2),
            lay((d, d)), lay((1, d)), lay((1, d)),
            lay((d, 2 * D_FF)), lay((D_FF, d)), lay((1, d)), lay((1, d)),
        ],
        out_specs=pl.BlockSpec((1, ts, d), lambda b, t: (b, t, 0)),
        out_shape=jax.ShapeDtypeStruct((bsz, seq, d), F32),
        compiler_params=pltpu.CompilerParams(dimension_semantics=("parallel", "parallel"),
                                             vmem_limit_bytes=VMEM_LIMIT),
        name="out_ffn",
    )(x, yc, yr, mod, mod, mod, mod, wout, g1, b1, wfi, wfo, g2, b2)


def kernel(x, c, w_mod, b_mod, w_in, conv_w, mu_shift, w0, w_decay_up, a0, a_up, g_up, k_k, k_a, r_k,
           lnx_g, lnx_b, w_out, ln1_g, ln1_b, w_ffn_in, w_ffn_out, ln2_g, ln2_b):
    depth = w_mod.shape[0]
    bsz, _, d = x.shape
    mod = _modulation(c, w_mod, b_mod).reshape(depth, bsz, N_MOD, 1, d)

    head = jnp.arange(RWKV_WIDTH) // HEAD_SIZE
    ones_bd = (head[:, None] == head[None, :]).astype(BF16)
    zeros_half = jnp.zeros((depth, LORA_WA // 2, RWKV_WIDTH), F32)
    wd = jnp.concatenate([w_decay_up, zeros_half], axis=1).astype(BF16)
    au = jnp.concatenate([zeros_half, a_up], axis=1).astype(BF16)
    vec = lambda p: p.reshape(depth, 1, -1)
    win, gu, wout, wfi, wfo = (p.astype(BF16) for p in (w_in, g_up, w_out, w_ffn_in, w_ffn_out))
    mu, w0_, a0_, kk_, ka_, rk_, lng, lnb = (vec(p) for p in (mu_shift, w0, a0, k_k, k_a, r_k, lnx_g, lnx_b))
    g1, b1, g2, b2 = (vec(p) for p in (ln1_g, ln1_b, ln2_g, ln2_b))

    for l in range(depth):
        yc, r, lw, k, v, kkn, bv, g = _inproj(l, x, mod, win, conv_w, mu, w0_, a0_, kk_, ka_, wd, au, gu, ones_bd)
        yr = _rwkv(l, r, lw, k, v, kkn, bv, g, rk_, lng, lnb, ones_bd)
        x = _out_ffn(l, x, yc, yr, mod, wout, g1, b1, wfi, wfo, g2, b2)
    return x
```
